```python
import jax, jax.numpy as jnp
from jax import lax
import numpy as np

D_MODEL = 1024
BATCH = 8
SEQ = 2048
DEPTH = 1
DEC_BATCH = 128
DEC_SEQ = 4
PAST_LEN = 16384
PAGE_SIZE = 128

LRU_WIDTH = D_MODEL
LRU_HEADS = 16
LRU_HEAD_DIM = LRU_WIDTH // LRU_HEADS
CONV_WIDTH = 4
LRU_C = 8.0
POOL_WIDTH = D_MODEL
POOL_WINDOWS = (2, 4, 8, 16)
POOL_GROUPS = len(POOL_WINDOWS)
POOL_GROUP_DIM = POOL_WIDTH // POOL_GROUPS
POOL_BUF = max(POOL_WINDOWS) - 1
N_EXPERTS = 32
TOP_K = 4
D_EXPERT = D_MODEL
SWIGLU_LIMIT = 7.0
SWIGLU_ALPHA = 1.702
PLE_DIM = 256
EPS = 1e-6
IN_COLS = LRU_WIDTH + POOL_WIDTH + 2 * D_MODEL

kernel_name = "hybrid_rglru_pool_moe_decode_step"


def rmsnorm(x, g):
    xf = x.astype(jnp.float32)
    y = xf * lax.rsqrt(jnp.mean(xf * xf, axis=-1, keepdims=True) + EPS)
    return (y * g.astype(jnp.float32)).astype(x.dtype)


def causal_conv(xa, buf, w, b):
    L = xa.shape[1]
    ext = jnp.concatenate([buf.astype(xa.dtype), xa], axis=1)
    out = b
    for k in range(CONV_WIDTH):
        out = out + ext[:, k:k + L] * w[k]
    return out, ext[:, -(CONV_WIDTH - 1):]


def rg_lru(xc, h0, pos, w_ra, b_ra, w_rx, b_rx, lam):
    B, L, W = xc.shape
    xh = xc.reshape(B, L, LRU_HEADS, LRU_HEAD_DIM)
    r = jax.nn.sigmoid((jnp.einsum('blhi,hij->blhj', xh, w_ra).reshape(B, L, W) + b_ra).astype(jnp.float32))
    ig = jax.nn.sigmoid((jnp.einsum('blhi,hij->blhj', xh, w_rx).reshape(B, L, W) + b_rx).astype(jnp.float32))
    log_a = -LRU_C * r * jax.nn.softplus(-lam.astype(jnp.float32))
    a = jnp.exp(log_a)
    mult = jnp.where((pos == 0)[None, :, None], 1.0, jnp.sqrt(-jnp.expm1(2.0 * log_a)))
    bterm = mult * ig * xc.astype(jnp.float32)
    bterm = bterm.at[:, 0].add(a[:, 0] * h0.astype(jnp.float32))

    def combine(lhs, rhs):
        a1, b1 = lhs
        a2, b2 = rhs
        return a1 * a2, a2 * b1 + b2

    _, h = lax.associative_scan(combine, (a, bterm), axis=1)
    return h.astype(xc.dtype), h[:, -1]


def pool_mix(xb, buf, pos, pool_w, pool_scale):
    B, L, W = xb.shape
    ext = jnp.concatenate([buf.astype(xb.dtype), xb], axis=1)
    extf = ext.astype(jnp.float32)
    cs = jnp.concatenate([jnp.zeros((B, 1, W), jnp.float32), jnp.cumsum(extf, axis=1)], axis=1)
    xbf = xb.astype(jnp.float32)
    outs = []
    for g, win in enumerate(POOL_WINDOWS):
        sl = slice(g * POOL_GROUP_DIM, (g + 1) * POOL_GROUP_DIM)
        tot = cs[:, POOL_BUF + 1:POOL_BUF + 1 + L, sl] - cs[:, POOL_BUF + 1 - win:POOL_BUF + 1 - win + L, sl]
        count = jnp.minimum(win, pos + 1).astype(jnp.float32)
        mixed = (tot / count[None, :, None] - xbf[..., sl]).astype(xb.dtype)
        outs.append(jnp.einsum('bld,de->ble', mixed, pool_w[g]))
    y = jnp.concatenate(outs, axis=-1) * pool_scale
    return y, ext[:, -POOL_BUF:]


def moe(u, w_router, b_router, w_gate_up, b_gate_up, w_down, b_down):
    B, L, D = u.shape
    t = u.reshape(B * L, D)
    logits = (t @ w_router + b_router).astype(jnp.float32)
    vals, idx = lax.top_k(logits, TOP_K)
    wts = jax.nn.softmax(vals, axis=-1)
    comb = jnp.einsum('tk,tke->te', wts, jax.nn.one_hot(idx, N_EXPERTS, dtype=jnp.float32)).astype(u.dtype)
    out = jnp.zeros_like(t)
    for e in range(N_EXPERTS):
        gu = t @ w_gate_up[e] + b_gate_up[e]
        gate = jnp.minimum(gu[:, :D_EXPERT], SWIGLU_LIMIT)
        up = jnp.clip(gu[:, D_EXPERT:], -SWIGLU_LIMIT, SWIGLU_LIMIT)
        h = (up + 1.0) * (gate * jax.nn.sigmoid(SWIGLU_ALPHA * gate))
        out = out + comb[:, e:e + 1] * (h @ w_down[e] + b_down[e])
    return out.reshape(B, L, D)


def layer(x, p, h0, conv_buf, pool_buf, pos0, i, g_mix, w_in, conv_w, conv_b, w_rg_a, b_rg_a, w_rg_x, b_rg_x,
          lru_lambda, pool_w, pool_scale, w_proj_a, w_proj_b, w_out, g_moe, w_router, b_router,
          w_gate_up, b_gate_up, w_down, b_down, g_ple, w_ple_gate, w_ple, g_ple_post):
    L = x.shape[1]
    pos = pos0 + jnp.arange(L, dtype=jnp.int32)
    u = rmsnorm(x, g_mix[i])
    z = u @ w_in[i]
    xa = z[..., :LRU_WIDTH]
    xb = z[..., LRU_WIDTH:LRU_WIDTH + POOL_WIDTH]
    ga = z[..., LRU_WIDTH + POOL_WIDTH:LRU_WIDTH + POOL_WIDTH + D_MODEL]
    gb = z[..., LRU_WIDTH + POOL_WIDTH + D_MODEL:]
    xc, conv_new = causal_conv(xa, conv_buf, conv_w[i], conv_b[i])
    ya, h_last = rg_lru(xc, h0, pos, w_rg_a[i], b_rg_a[i], w_rg_x[i], b_rg_x[i], lru_lambda[i])
    yb, pool_new = pool_mix(xb, pool_buf, pos, pool_w[i], pool_scale[i])
    merged = jax.nn.sigmoid(ga) * (ya @ w_proj_a[i]) + jax.nn.sigmoid(gb) * (yb @ w_proj_b[i])
    x = x + merged @ w_out[i]
    x = x + moe(rmsnorm(x, g_moe[i]), w_router[i], b_router[i], w_gate_up[i], b_gate_up[i], w_down[i], b_down[i])
    ple = rmsnorm(p @ w_ple[i], g_ple_post[i]) * jax.nn.sigmoid(rmsnorm(x, g_ple[i]) @ w_ple_gate[i])
    x = x + ple
    return x, h_last, conv_new, pool_new


def setup_inputs(seed: int = 0) -> dict:
    key = jax.random.key(seed)
    ks = iter(jax.random.split(key, 40))
    f32 = jnp.float32

    def nrm(shape, scale):
        return jax.random.normal(next(ks), shape, f32) * scale

    def gain(shape):
        return 1.0 + nrm(shape, 0.05)

    u = jax.random.uniform(next(ks), (DEPTH, LRU_WIDTH), f32, 0.9, 0.999)
    s = u ** (1.0 / LRU_C)
    lam = jnp.log(s) - jnp.log1p(-s)
    return {
        "x_prompt": nrm((BATCH, SEQ, D_MODEL), 1.0),
        "x_sample": nrm((DEC_BATCH, DEC_SEQ, D_MODEL), 1.0),
        "p_prompt": nrm((DEPTH, BATCH, SEQ, PLE_DIM), 1.0),
        "p_sample": nrm((DEPTH, DEC_BATCH, DEC_SEQ, PLE_DIM), 1.0),
        "state_lru_h": nrm((DEPTH, DEC_BATCH, LRU_WIDTH), 0.5),
        "state_conv": nrm((DEPTH, DEC_BATCH, CONV_WIDTH - 1, LRU_WIDTH), 1.0),
        "state_pool": nrm((DEPTH, DEC_BATCH, POOL_BUF, POOL_WIDTH), 1.0),
        "g_mix": gain((DEPTH, D_MODEL)),
        "w_in": nrm((DEPTH, D_MODEL, IN_COLS), D_MODEL ** -0.5),
        "conv_w": nrm((DEPTH, CONV_WIDTH, LRU_WIDTH), CONV_WIDTH ** -0.5),
        "conv_b": nrm((DEPTH, LRU_WIDTH), 0.01),
        "w_rg_a": nrm((DEPTH, LRU_HEADS, LRU_HEAD_DIM, LRU_HEAD_DIM), LRU_HEAD_DIM ** -0.5),
        "b_rg_a": nrm((DEPTH, LRU_WIDTH), 0.01),
        "w_rg_x": nrm((DEPTH, LRU_HEADS, LRU_HEAD_DIM, LRU_HEAD_DIM), LRU_HEAD_DIM ** -0.5),
        "b_rg_x": nrm((DEPTH, LRU_WIDTH), 0.01),
        "lru_lambda": lam,
        "pool_w": nrm((DEPTH, POOL_GROUPS, POOL_GROUP_DIM, POOL_GROUP_DIM), POOL_GROUP_DIM ** -0.5),
        "pool_scale": gain((DEPTH, POOL_WIDTH)),
        "w_proj_a": nrm((DEPTH, LRU_WIDTH, D_MODEL), LRU_WIDTH ** -0.5),
        "w_proj_b": nrm((DEPTH, POOL_WIDTH, D_MODEL), POOL_WIDTH ** -0.5),
        "w_out": nrm((DEPTH, D_MODEL, D_MODEL), D_MODEL ** -0.5),
        "g_moe": gain((DEPTH, D_MODEL)),
        "w_router": nrm((DEPTH, D_MODEL, N_EXPERTS), D_MODEL ** -0.5),
        "b_router": nrm((DEPTH, N_EXPERTS), 0.01),
        "w_gate_up": nrm((DEPTH, N_EXPERTS, D_MODEL, 2 * D_EXPERT), D_MODEL ** -0.5),
        "b_gate_up": nrm((DEPTH, N_EXPERTS, 2 * D_EXPERT), 0.01),
        "w_down": nrm((DEPTH, N_EXPERTS, D_EXPERT, D_MODEL), D_EXPERT ** -0.5),
        "b_down": nrm((DEPTH, N_EXPERTS, D_MODEL), 0.01),
        "g_ple": gain((DEPTH, D_MODEL)),
        "w_ple_gate": nrm((DEPTH, D_MODEL, D_MODEL), D_MODEL ** -0.5),
        "w_ple": nrm((DEPTH, PLE_DIM, D_MODEL), PLE_DIM ** -0.5),
        "g_ple_post": gain((DEPTH, D_MODEL)),
        "g_final": gain((D_MODEL,)),
    }


def reference(x_prompt, x_sample, p_prompt, p_sample, state_lru_h, state_conv, state_pool,
              g_mix, w_in, conv_w, conv_b, w_rg_a, b_rg_a, w_rg_x, b_rg_x, lru_lambda,
              pool_w, pool_scale, w_proj_a, w_proj_b, w_out, g_moe, w_router, b_router,
              w_gate_up, b_gate_up, w_down, b_down, g_ple, w_ple_gate, w_ple, g_ple_post, g_final):
    bp = x_prompt.shape[0]
    xp, xs = x_prompt, x_sample
    hp_l, cp_l, pp_l, hs_l, cs_l, ps_l = [], [], [], [], [], []
    for i in range(DEPTH):
        h0p = jnp.zeros((bp, LRU_WIDTH), jnp.float32)
        cb0p = jnp.zeros((bp, CONV_WIDTH - 1, LRU_WIDTH), xp.dtype)
        pb0p = jnp.zeros((bp, POOL_BUF, POOL_WIDTH), xp.dtype)
        xp, hp, cp, pp = layer(xp, p_prompt[i], h0p, cb0p, pb0p, 0, i, g_mix, w_in, conv_w, conv_b,
                               w_rg_a, b_rg_a, w_rg_x, b_rg_x, lru_lambda, pool_w, pool_scale,
                               w_proj_a, w_proj_b, w_out, g_moe, w_router, b_router, w_gate_up,
                               b_gate_up, w_down, b_down, g_ple, w_ple_gate, w_ple, g_ple_post)
        xs, hs, cs, ps = layer(xs, p_sample[i], state_lru_h[i], state_conv[i], state_pool[i], PAST_LEN, i,
                               g_mix, w_in, conv_w, conv_b, w_rg_a, b_rg_a, w_rg_x, b_rg_x, lru_lambda,
                               pool_w, pool_scale, w_proj_a, w_proj_b, w_out, g_moe, w_router, b_router,
                               w_gate_up, b_gate_up, w_down, b_down, g_ple, w_ple_gate, w_ple, g_ple_post)
        hp_l.append(hp.astype(x_prompt.dtype))
        cp_l.append(cp.astype(x_prompt.dtype))
        pp_l.append(pp.astype(x_prompt.dtype))
        hs_l.append(hs.astype(state_lru_h.dtype))
        cs_l.append(cs.astype(state_conv.dtype))
        ps_l.append(ps.astype(state_pool.dtype))
    y_prompt = rmsnorm(xp, g_final)
    y_sample = rmsnorm(xs, g_final)
    h_prompt = jnp.stack(hp_l)
    conv_prompt = jnp.stack(cp_l)
    pool_prompt = jnp.stack(pp_l)
    h_sample = jnp.stack(hs_l)
    conv_sample = jnp.stack(cs_l)
    pool_sample = jnp.stack(ps_l)
    return (y_prompt, y_sample, h_prompt, conv_prompt, pool_prompt, h_sample, conv_sample, pool_sample)
```

```python
import functools

import jax
import jax.numpy as jnp
from jax import lax
from jax.experimental import pallas as pl
from jax.experimental.pallas import tpu as pltpu

F32 = jnp.float32
BF16 = jnp.bfloat16
I32 = jnp.int32

EPS = 1e-6
LRU_C = 8.0
LRU_HEADS = 16
CONV_WIDTH = 4
POOL_WINDOWS = (2, 4, 8, 16)
POOL_BUF = max(POOL_WINDOWS) - 1
N_EXPERTS = 32
TOP_K = 4
SWIGLU_LIMIT = 7.0
SWIGLU_ALPHA = 1.702
PAST_LEN = 16384

V7X_LANES = 128
V7X_SUBLANES = 8
V7X_VMEM_BYTES = 64 * 1024 * 1024
VMEM_LIMIT = V7X_VMEM_BYTES - 8 * 1024 * 1024

GATE_GROUP = 256
ROW_CHUNKS = 8


def _rmsnorm(x, g):
    ms = jnp.mean(x * x, axis=-1, keepdims=True)
    return x * lax.rsqrt(ms + EPS) * g


def _sigmoid(x):
    return 1.0 / (1.0 + jnp.exp(-x))


def _softplus(x):
    return jnp.maximum(x, 0.0) + jnp.log1p(jnp.exp(-jnp.abs(x)))


def _dot(a, b):
    return jnp.dot(a, b, preferred_element_type=F32)


def _in_proj(x, g, w_in):
    return _dot(_rmsnorm(x, g).astype(BF16), w_in)


def _gates(xc, wbd_ref, bra, brx):
    xcb = xc.astype(BF16)
    n_groups = xc.shape[1] // GATE_GROUP
    rs, gs = [], []
    for g in range(n_groups):
        o = _dot(xcb[:, g * GATE_GROUP:(g + 1) * GATE_GROUP], wbd_ref[g])
        rs.append(o[:, :GATE_GROUP])
        gs.append(o[:, GATE_GROUP:])
    r = _sigmoid(jnp.concatenate(rs, axis=-1) + bra)
    ig = _sigmoid(jnp.concatenate(gs, axis=-1) + brx)
    return r, ig


def _lru_coeffs(r, lam):
    log_a = (-LRU_C * _softplus(-lam)) * r
    a = jnp.exp(log_a)
    th = jnp.tanh(log_a)
    mult = jnp.sqrt(-2.0 * th / (1.0 - th))
    return a, mult


def _scan_rows(a, b, h0):
    tl, w = a.shape
    groups = tl // V7X_SUBLANES
    a3 = a.reshape(groups, V7X_SUBLANES, w)
    b3 = b.reshape(groups, V7X_SUBLANES, w)
    sub = lax.broadcasted_iota(I32, (groups, V7X_SUBLANES, w), 1)
    for s in (1, 2, 4):
        a_sh = pltpu.roll(a3, s, 1)
        b_sh = pltpu.roll(b3, s, 1)
        valid = sub >= s
        b3 = jnp.where(valid, a3 * b_sh + b3, b3)
        a3 = jnp.where(valid, a3 * a_sh, a3)
    hs = []
    h = h0
    for g in range(groups):
        hg = a3[g] * h + b3[g]
        hs.append(hg)
        h = hg[V7X_SUBLANES - 1:V7X_SUBLANES, :]
    return jnp.concatenate(hs, axis=0), h


def _merge_out(x, h, yb, ga, gb, wpa, wpb, wout):
    pa = _dot(h.astype(BF16), wpa)
    pb = _dot(yb.astype(BF16), wpb)
    merged = _sigmoid(ga) * pa + _sigmoid(gb) * pb
    return x + _dot(merged.astype(BF16), wout)


def _pool_proj(mixed, poolw_ref, pscale):
    gd = mixed.shape[1] // len(POOL_WINDOWS)
    mb = mixed.astype(BF16)
    outs = [_dot(mb[:, g * gd:(g + 1) * gd], poolw_ref[g]) for g in range(len(POOL_WINDOWS))]
    return jnp.concatenate(outs, axis=-1) * pscale


def _mixer_prompt_kernel(x_ref, gmix_ref, win_ref, convw_ref, convb_ref, wbd_ref, bra_ref, brx_ref, lam_ref,
                         poolw_ref, pscale_ref, wpa_ref, wpb_ref, wout_ref,
                         x1_ref, h_ref, conv_ref, pool_ref,
                         hc_ref, cc_ref, pc_ref, *, tl, n_tiles):
    j = pl.program_id(1)
    d = x_ref.shape[-1]

    @pl.when(j == 0)
    def _():
        hc_ref[...] = jnp.zeros_like(hc_ref)
        cc_ref[...] = jnp.zeros_like(cc_ref)
        pc_ref[...] = jnp.zeros_like(pc_ref)

    x = x_ref[...]
    z = _in_proj(x, gmix_ref[...], win_ref[...])
    xa, xb, ga, gb = z[:, :d], z[:, d:2 * d], z[:, 2 * d:3 * d], z[:, 3 * d:]
    row = lax.broadcasted_iota(I32, (tl, 1), 0)

    full = jnp.concatenate([cc_ref[...], xa], axis=0)
    cw = convw_ref[...]
    xc = convb_ref[...]
    for k in range(CONV_WIDTH):
        s = CONV_WIDTH - 1 - k
        term = xa if s == 0 else pltpu.roll(full, s, 0)[V7X_SUBLANES:]
        xc = xc + term * cw[k:k + 1]

    r, ig = _gates(xc, wbd_ref, bra_ref[...], brx_ref[...])
    a, mult = _lru_coeffs(r, lam_ref[...])
    mult = jnp.where(jnp.logical_and(j == 0, row == 0), 1.0, mult)
    bterm = mult * ig * xc
    h, h_last = _scan_rows(a, bterm, hc_ref[0:1, :])

    gd = d // len(POOL_WINDOWS)
    ext = jnp.concatenate([pc_ref[...], xb], axis=0)
    pad = pc_ref.shape[0]
    s2 = ext + pltpu.roll(ext, 1, 0)
    s4 = s2[:, gd:] + pltpu.roll(s2[:, gd:], 2, 0)
    s8 = s4[:, gd:] + pltpu.roll(s4[:, gd:], 4, 0)
    s16 = s8[:, gd:] + pltpu.roll(s8[:, gd:], 8, 0)
    tots = [s2[pad:, :gd], s4[pad:, :gd], s8[pad:, :gd], s16[pad:, :]]
    pos = j * tl + row
    mixed = []
    for g, win in enumerate(POOL_WINDOWS):
        inv = 1.0 / jnp.minimum(win, pos + 1).astype(F32)
        mixed.append(tots[g] * inv - xb[:, g * gd:(g + 1) * gd])
    yb = _pool_proj(jnp.concatenate(mixed, axis=-1), poolw_ref, pscale_ref[...])

    x1_ref[...] = _merge_out(x, h, yb, ga, gb, wpa_ref[...], wpb_ref[...], wout_ref[...])

    hc_ref[...] = jnp.broadcast_to(h_last, hc_ref.shape)
    cc_ref[...] = xa[tl - cc_ref.shape[0]:, :]
    pc_ref[...] = xb[tl - pc_ref.shape[0]:, :]

    @pl.when(j == n_tiles - 1)
    def _():
        h_ref[0] = h_last
        conv_ref[0] = xa[tl - (CONV_WIDTH - 1):, :]
        pool_ref[0] = xb[tl - POOL_BUF:, :]


def _const_spec(shape):
    nd = len(shape)
    return pl.BlockSpec(shape, lambda *_: (0,) * nd, pipeline_mode=pl.Buffered(1))


def _mixer_weight_specs(wts):
    return [_const_spec(w.shape) for w in wts]


def _mixer_prompt(x, wts, tl):
    nb, seq, d = x.shape
    n_tiles = seq // tl
    kern = functools.partial(_mixer_prompt_kernel, tl=tl, n_tiles=n_tiles)
    out_shape = (
        jax.ShapeDtypeStruct((nb * seq, d), F32),
        jax.ShapeDtypeStruct((nb, 1, d), F32),
        jax.ShapeDtypeStruct((nb, CONV_WIDTH - 1, d), F32),
        jax.ShapeDtypeStruct((nb, POOL_BUF, d), F32),
    )
    return pl.pallas_call(
        kern,
        grid=(nb, n_tiles),
        in_specs=[pl.BlockSpec((None, tl, d), lambda b, j: (b, j, 0))] + _mixer_weight_specs(wts),
        out_specs=(
            pl.BlockSpec((tl, d), lambda b, j: (b * n_tiles + j, 0)),
            pl.BlockSpec((1, 1, d), lambda b, j: (b, 0, 0)),
            pl.BlockSpec((1, CONV_WIDTH - 1, d), lambda b, j: (b, 0, 0)),
            pl.BlockSpec((1, POOL_BUF, d), lambda b, j: (b, 0, 0)),
        ),
        out_shape=out_shape,
        scratch_shapes=[
            pltpu.VMEM((V7X_SUBLANES, d), F32),
            pltpu.VMEM((V7X_SUBLANES, d), F32),
            pltpu.VMEM((2 * V7X_SUBLANES, d), F32),
        ],
        compiler_params=pltpu.CompilerParams(
            dimension_semantics=("arbitrary", "arbitrary"), vmem_limit_bytes=VMEM_LIMIT),
        name="mixer_prompt",
    )(x, *wts)


def _mixer_sample_kernel(x_ref, h0_ref, cbuf_ref, pbuf_ref,
                         gmix_ref, win_ref, convw_ref, convb_ref, wbd_ref, bra_ref, brx_ref, lam_ref,
                         poolw_ref, pscale_ref, wpa_ref, wpb_ref, wout_ref,
                         x1_ref, h_ref, conv_ref, pool_ref, *, sl):
    sb, d = h0_ref.shape
    xs = [x_ref[:, l * d:(l + 1) * d] for l in range(sl)]
    x = jnp.concatenate(xs, axis=0)
    z = _in_proj(x, gmix_ref[...], win_ref[...])
    rows = lambda v, l: v[l * sb:(l + 1) * sb]
    xa = [rows(z[:, :d], l) for l in range(sl)]
    xb = [rows(z[:, d:2 * d], l) for l in range(sl)]
    ga, gb = z[:, 2 * d:3 * d], z[:, 3 * d:]

    cext = [cbuf_ref[:, k * d:(k + 1) * d] for k in range(CONV_WIDTH - 1)] + xa
    cw = convw_ref[...]
    xcs = []
    for l in range(sl):
        acc = convb_ref[...]
        for k in range(CONV_WIDTH):
            acc = acc + cext[l + k] * cw[k:k + 1]
        xcs.append(acc)
    xc = jnp.concatenate(xcs, axis=0)
    for k in range(CONV_WIDTH - 1):
        conv_ref[:, k * d:(k + 1) * d] = cext[sl + k]

    r, ig = _gates(xc, wbd_ref, bra_ref[...], brx_ref[...])
    a, mult = _lru_coeffs(r, lam_ref[...])
    bterm = mult * ig * xc
    h = h0_ref[...]
    hs = []
    for l in range(sl):
        bl = rows(bterm, l)
        if PAST_LEN + l == 0:
            bl = rows(ig * xc, l)
        h = rows(a, l) * h + bl
        hs.append(h)
    h_ref[...] = h

    gd = d // len(POOL_WINDOWS)
    pext = [pbuf_ref[:, k * d:(k + 1) * d] for k in range(POOL_BUF)] + xb
    for k in range(POOL_BUF):
        pool_ref[:, k * d:(k + 1) * d] = pext[sl + k]
    mixed_rows = []
    for l in range(sl):
        parts = []
        for g, win in enumerate(POOL_WINDOWS):
            sl_g = slice(g * gd, (g + 1) * gd)
            tot = pext[POOL_BUF + l][:, sl_g]
            for jj in range(1, win):
                tot = tot + pext[POOL_BUF + l - jj][:, sl_g]
            cnt = float(min(win, PAST_LEN + l + 1))
            parts.append(tot / cnt - xb[l][:, sl_g])
        mixed_rows.append(jnp.concatenate(parts, axis=-1))
    yb = _pool_proj(jnp.concatenate(mixed_rows, axis=0), poolw_ref, pscale_ref[...])

    x1_ref[...] = _merge_out(x, jnp.concatenate(hs, axis=0), yb, ga, gb,
                             wpa_ref[...], wpb_ref[...], wout_ref[...])


def _mixer_sample(x2d, h0, cbuf2d, pbuf2d, wts, sl):
    sb, d = h0.shape
    kern = functools.partial(_mixer_sample_kernel, sl=sl)
    ins = (x2d, h0, cbuf2d, pbuf2d) + tuple(wts)
    out_shape = (
        jax.ShapeDtypeStruct((sl * sb, d), F32),
        jax.ShapeDtypeStruct((sb, d), F32),
        jax.ShapeDtypeStruct(cbuf2d.shape, F32),
        jax.ShapeDtypeStruct(pbuf2d.shape, F32),
    )
    full = lambda s: pl.BlockSpec(s, lambda i: (0,) * len(s))
    return pl.pallas_call(
        kern,
        grid=(1,),
        in_specs=[full(v.shape) for v in ins],
        out_specs=tuple(full(s.shape) for s in out_shape),
        out_shape=out_shape,
        compiler_params=pltpu.CompilerParams(
            dimension_semantics=("arbitrary",), vmem_limit_bytes=VMEM_LIMIT),
        name="mixer_sample",
    )(*ins)


def _two_source_specs(tm, width, n_p_tiles):
    return [
        pl.BlockSpec((tm, width), lambda i, *_: (jnp.minimum(i, n_p_tiles - 1), 0)),
        pl.BlockSpec((tm, width), lambda i, *_: (jnp.maximum(i - n_p_tiles, 0), 0)),
    ]


def _router_kernel(xp_ref, xs_ref, g_ref, whi_ref, wlo_ref, br_ref, tri_ref,
                   idx_ref, rank_ref, wtok_ref, cnt_ref, base_ref, *, n_p_tiles):
    i = pl.program_id(0)
    tm = xp_ref.shape[0]

    @pl.when(i == 0)
    def _():
        base_ref[...] = jnp.zeros_like(base_ref)

    x = jnp.where(i < n_p_tiles, xp_ref[...], xs_ref[...])
    u = _rmsnorm(x, g_ref[...])
    u_hi = u.astype(BF16)
    u_lo = (u - u_hi.astype(F32)).astype(BF16)
    logits = _dot(u_hi, whi_ref[...]) + (_dot(u_hi, wlo_ref[...]) + _dot(u_lo, whi_ref[...]))
    lt = (logits + br_ref[...]).T[:N_EXPERTS]

    eio = lax.broadcasted_iota(I32, (N_EXPERTS, tm), 0)
    vals, idxs, sels = [], [], []
    cur = lt
    for _ in range(TOP_K):
        m = jnp.max(cur, axis=0, keepdims=True)
        ik = jnp.min(jnp.where(cur == m, eio, N_EXPERTS), axis=0, keepdims=True)
        sel = eio == ik
        vals.append(m)
        idxs.append(ik)
        sels.append(sel)
        cur = jnp.where(sel, -jnp.inf, cur)
    es = [jnp.exp(v - vals[0]) for v in vals]
    den = es[0]
    for e in es[1:]:
        den = den + e
    ws = [e / den for e in es]

    multi = sels[0].astype(F32)
    for s in sels[1:]:
        multi = multi + s.astype(F32)
    before = _dot(multi.astype(BF16), tri_ref[...]) + base_ref[:, 0:1]
    ranks = [jnp.sum(jnp.where(s, before, 0.0), axis=0, keepdims=True).astype(I32) for s in sels]

    idx_ref[...] = jnp.concatenate(idxs, axis=0)
    rank_ref[...] = jnp.concatenate(ranks, axis=0)
    wpad = jnp.concatenate(ws + [jnp.zeros((V7X_LANES - TOP_K, tm), F32)], axis=0)
    wtok_ref[...] = wpad.T
    new_base = base_ref[...] + jnp.sum(multi, axis=1, keepdims=True)
    base_ref[...] = new_base
    cnt_ref[...] = new_base.astype(I32)


def _router(x1p, x1s, g, whi, wlo, br, tm):
    t_p, d = x1p.shape
    t_s = x1s.shape[0]
    n_p_tiles, n_s_tiles = t_p // tm, t_s // tm
    t = t_p + t_s
    tri = (lax.broadcasted_iota(I32, (tm, tm), 0) < lax.broadcasted_iota(I32, (tm, tm), 1)).astype(BF16)
    kern = functools.partial(_router_kernel, n_p_tiles=n_p_tiles)
    out_shape = (
        jax.ShapeDtypeStruct((TOP_K, t), I32),
        jax.ShapeDtypeStruct((TOP_K, t), I32),
        jax.ShapeDtypeStruct((t, V7X_LANES), F32),
        jax.ShapeDtypeStruct((N_EXPERTS, V7X_LANES), I32),
    )
    consts = (g, whi, wlo, br, tri)
    return pl.pallas_call(
        kern,
        grid=(n_p_tiles + n_s_tiles,),
        in_specs=_two_source_specs(tm, d, n_p_tiles) + [_const_spec(c.shape) for c in consts],
        out_specs=(
            pl.BlockSpec((TOP_K, tm), lambda i: (0, i)),
            pl.BlockSpec((TOP_K, tm), lambda i: (0, i)),
            pl.BlockSpec((tm, V7X_LANES), lambda i: (i, 0)),
            pl.BlockSpec((N_EXPERTS, V7X_LANES), lambda i: (0, 0)),
        ),
        out_shape=out_shape,
        scratch_shapes=[pltpu.VMEM((N_EXPERTS, V7X_LANES), F32)],
        compiler_params=pltpu.CompilerParams(
            dimension_semantics=("arbitrary",), vmem_limit_bytes=VMEM_LIMIT),
        name="router",
    )(x1p, x1s, *consts)


def _row_copy(src, src_row, dst, dst_row, sem):
    return pltpu.make_async_copy(
        src.at[pl.ds(pl.multiple_of(src_row * ROW_CHUNKS, ROW_CHUNKS), ROW_CHUNKS), :],
        dst.at[pl.ds(pl.multiple_of(dst_row * ROW_CHUNKS, ROW_CHUNKS), ROW_CHUNKS), :],
        sem)


def _to_row_tiles(dst_ref, base, val):
    n = val.shape[0]
    for c in range(ROW_CHUNKS):
        dst_ref[pl.ds(base + c, n, stride=ROW_CHUNKS), :] = val[:, c * V7X_LANES:(c + 1) * V7X_LANES]


def _from_row_tiles(src_ref, base, n):
    return jnp.concatenate(
        [src_ref[pl.ds(base + c, n, stride=ROW_CHUNKS), :] for c in range(ROW_CHUNKS)], axis=-1)


def _dispatch_kernel(pos_ref, xp_ref, xs_ref, g_ref, out_hbm, slab_ref, sem, *, n_p_tiles):
    i = pl.program_id(0)
    td = xp_ref.shape[0]
    x = jnp.where(i < n_p_tiles, xp_ref[...], xs_ref[...])
    _to_row_tiles(slab_ref, 0, _rmsnorm(x, g_ref[...]))

    def issue(t, carry):
        for k in range(TOP_K):
            _row_copy(slab_ref, t, out_hbm, pos_ref[k, t], sem).start()
        return carry

    lax.fori_loop(0, td, issue, 0)
    for _ in range(TOP_K):
        pltpu.make_async_copy(slab_ref, out_hbm.at[pl.ds(0, td * ROW_CHUNKS), :], sem).wait()


def _dispatch(pos, x1p, x1s, g, td):
    t_p, d = x1p.shape
    t_s = x1s.shape[0]
    n_p_tiles, n_s_tiles = t_p // td, t_s // td
    t = t_p + t_s
    kern = functools.partial(_dispatch_kernel, n_p_tiles=n_p_tiles)
    return pl.pallas_call(
        kern,
        grid=(n_p_tiles + n_s_tiles,),
        in_specs=[pl.BlockSpec((TOP_K, td), lambda i: (0, i), memory_space=pltpu.SMEM)]
        + _two_source_specs(td, d, n_p_tiles) + [_const_spec(g.shape)],
        out_specs=pl.BlockSpec(memory_space=pl.ANY),
        out_shape=jax.ShapeDtypeStruct((TOP_K * t * ROW_CHUNKS, V7X_LANES), F32),
        scratch_shapes=[pltpu.VMEM((td * ROW_CHUNKS, V7X_LANES), F32), pltpu.SemaphoreType.DMA],
        compiler_params=pltpu.CompilerParams(
            dimension_semantics=("arbitrary",), vmem_limit_bytes=VMEM_LIMIT),
        name="dispatch",
    )(pos, x1p, x1s, g)


def _experts_kernel(tile_ref, exp_ref, lo_ref, hi_ref, first_ref, wchg_ref,
                    xs_ref, wgu_ref, bgu_ref, wd_ref, bd_ref, ys_ref, wgu_s, wd_s, *, tmx):
    i = pl.program_id(0)
    de = wd_ref.shape[1]

    @pl.when(wchg_ref[i] == 1)
    def _():
        wgu_s[...] = wgu_ref[0].astype(BF16)
        wd_s[...] = wd_ref[0].astype(BF16)

    lo, hi = lo_ref[i], hi_ref[i]

    @pl.when(hi > lo)
    def _():
        x = _from_row_tiles(xs_ref, 0, tmx).astype(BF16)
        gu = _dot(x, wgu_s[...]) + bgu_ref[0]
        gate = jnp.minimum(gu[:, :de], SWIGLU_LIMIT)
        up = jnp.clip(gu[:, de:], -SWIGLU_LIMIT, SWIGLU_LIMIT)
        h = (up + 1.0) * (gate * _sigmoid(SWIGLU_ALPHA * gate))
        y = _dot(h.astype(BF16), wd_s[...]) + bd_ref[0]
        row = tile_ref[i] * tmx + lax.broadcasted_iota(I32, (tmx, 1), 0)
        mine = jnp.logical_and(row >= lo, row < hi)

        @pl.when(first_ref[i] == 1)
        def _():
            _to_row_tiles(ys_ref, 0, jnp.where(mine, y, 0.0))

        @pl.when(first_ref[i] == 0)
        def _():
            _to_row_tiles(ys_ref, 0, jnp.where(mine, y, _from_row_tiles(ys_ref, 0, tmx)))


def _experts(plan, xs, wgu, bgu, wd, bd, tmx):
    n_rows = xs.shape[0] // ROW_CHUNKS
    n_work = plan[0].shape[0]
    _, d, de2 = wgu.shape
    de = wd.shape[1]
    kern = functools.partial(_experts_kernel, tmx=tmx)
    grid_spec = pltpu.PrefetchScalarGridSpec(
        num_scalar_prefetch=len(plan),
        grid=(n_work,),
        in_specs=[
            pl.BlockSpec((tmx * ROW_CHUNKS, V7X_LANES), lambda i, tile, *_: (tile[i], 0)),
            pl.BlockSpec((1, d, de2), lambda i, tile, ex, *_: (ex[i], 0, 0)),
            pl.BlockSpec((1, 1, de2), lambda i, tile, ex, *_: (ex[i], 0, 0)),
            pl.BlockSpec((1, de, d), lambda i, tile, ex, *_: (ex[i], 0, 0)),
            pl.BlockSpec((1, 1, d), lambda i, tile, ex, *_: (ex[i], 0, 0)),
        ],
        out_specs=pl.BlockSpec((tmx * ROW_CHUNKS, V7X_LANES), lambda i, tile, *_: (tile[i], 0)),
        scratch_shapes=[pltpu.VMEM((d, de2), BF16), pltpu.VMEM((de, d), BF16)],
    )
    return pl.pallas_call(
        kern,
        grid_spec=grid_spec,
        out_shape=jax.ShapeDtypeStruct((n_rows * ROW_CHUNKS, V7X_LANES), F32),
        compiler_params=pltpu.CompilerParams(
            dimension_semantics=("arbitrary",), vmem_limit_bytes=VMEM_LIMIT),
        name="experts",
    )(*plan, xs, wgu, bgu, wd, bd)


def _combine_kernel(pos_ref, ys_hbm, xp_ref, xs_ref, pp_ref, ps_ref, wtok_ref,
                    gple_ref, wgate_ref, wple_ref, gpost_ref, gfin_ref,
                    yp_ref, ysm_ref, gath_ref, sem, *, n_p_tiles):
    i = pl.program_id(0)
    tc = xp_ref.shape[0]

    def issue(t, carry):
        for k in range(TOP_K):
            _row_copy(ys_hbm, pos_ref[k, t], gath_ref, k * tc + t, sem).start()
        return carry

    lax.fori_loop(0, tc, issue, 0)
    is_p = i < n_p_tiles
    x1 = jnp.where(is_p, xp_ref[...], xs_ref[...])
    p = jnp.where(is_p, pp_ref[...], ps_ref[...])
    ple = _rmsnorm(_dot(p.astype(BF16), wple_ref[...]), gpost_ref[...])
    pltpu.make_async_copy(ys_hbm.at[pl.ds(0, TOP_K * tc * ROW_CHUNKS), :], gath_ref, sem).wait()

    wt = wtok_ref[...]
    moe = wt[:, 0:1] * _from_row_tiles(gath_ref, 0, tc)
    for k in range(1, TOP_K):
        moe = moe + wt[:, k:k + 1] * _from_row_tiles(gath_ref, k * tc * ROW_CHUNKS, tc)
    x2 = x1 + moe
    gate = _sigmoid(_dot(_rmsnorm(x2, gple_ref[...]).astype(BF16), wgate_ref[...]))
    y = _rmsnorm(x2 + ple * gate, gfin_ref[...])

    @pl.when(is_p)
    def _():
        yp_ref[...] = y

    @pl.when(jnp.logical_not(is_p))
    def _():
        ysm_ref[...] = y


def _combine(pos, ys, x1p, x1s, pp, ps, wtok, consts, tc):
    t_p, d = x1p.shape
    t_s = x1s.shape[0]
    n_p_tiles, n_s_tiles = t_p // tc, t_s // tc
    kern = functools.partial(_combine_kernel, n_p_tiles=n_p_tiles)
    return pl.pallas_call(
        kern,
        grid=(n_p_tiles + n_s_tiles,),
        in_specs=[pl.BlockSpec((TOP_K, tc), lambda i: (0, i), memory_space=pltpu.SMEM),
                  pl.BlockSpec(memory_space=pl.ANY)]
        + _two_source_specs(tc, d, n_p_tiles) + _two_source_specs(tc, pp.shape[1], n_p_tiles)
        + [pl.BlockSpec((tc, V7X_LANES), lambda i: (i, 0))] + [_const_spec(c.shape) for c in consts],
        out_specs=tuple(_two_source_specs(tc, d, n_p_tiles)),
        out_shape=(jax.ShapeDtypeStruct((t_p, d), F32), jax.ShapeDtypeStruct((t_s, d), F32)),
        scratch_shapes=[pltpu.VMEM((TOP_K * tc * ROW_CHUNKS, V7X_LANES), F32), pltpu.SemaphoreType.DMA],
        compiler_params=pltpu.CompilerParams(
            dimension_semantics=("arbitrary",), vmem_limit_bytes=VMEM_LIMIT),
        name="combine",
    )(pos, ys, x1p, x1s, pp, ps, wtok, *consts)


def _plan(idx, rank, counts, tmx, n_work):
    ends = jnp.cumsum(counts)
    offs = ends - counts
    eids = jnp.arange(N_EXPERTS, dtype=I32)
    pos = rank + jnp.sum(jnp.where(idx[..., None] == eids, offs, 0), axis=-1)

    first_tile = offs // tmx
    last_tile = (ends - 1) // tmx
    n_e = jnp.where(counts > 0, last_tile - first_tile + 1, 0)
    iend = jnp.cumsum(n_e)
    istart = iend - n_e
    total = iend[-1]
    i = jnp.arange(n_work, dtype=I32)
    ic = jnp.minimum(i, total - 1)
    e_i = jnp.sum(ic[:, None] >= iend[None, :], axis=1).astype(I32)
    tile_i = first_tile[e_i] + (ic - istart[e_i])
    valid = i < total
    lo = jnp.where(valid, jnp.maximum(offs[e_i], tile_i * tmx), 0)
    hi = jnp.where(valid, jnp.minimum(ends[e_i], (tile_i + 1) * tmx), 0)
    prev_tile = jnp.concatenate([jnp.full((1,), -1, I32), tile_i[:-1]])
    prev_e = jnp.concatenate([jnp.full((1,), -1, I32), e_i[:-1]])
    first = (tile_i != prev_tile).astype(I32)
    wchg = (e_i != prev_e).astype(I32)
    return pos.astype(I32), tuple(v.astype(I32) for v in (tile_i, e_i, lo, hi, first, wchg))


def _block_diag_gates(w_a, w_x):
    heads, hd, _ = w_a.shape
    per = GATE_GROUP // hd
    groups = heads // per
    eye = jnp.eye(per, dtype=w_a.dtype)

    def bd(w):
        w4 = w.reshape(groups, per, hd, hd)
        return jnp.einsum('ghij,hk->ghikj', w4, eye).reshape(groups, GATE_GROUP, GATE_GROUP)

    return jnp.concatenate([bd(w_a), bd(w_x)], axis=-1).astype(BF16)


def _pick_tile(pref, *sizes):
    t = pref
    while any(s % t for s in sizes):
        t //= 2
    return t


def kernel(x_prompt, x_sample, p_prompt, p_sample, state_lru_h, state_conv, state_pool, g_mix, w_in, conv_w, conv_b, w_rg_a, b_rg_a, w_rg_x, b_rg_x, lru_lambda, pool_w, pool_scale, w_proj_a, w_proj_b, w_out, g_moe, w_router, b_router, w_gate_up, b_gate_up, w_down, b_down, g_ple, w_ple_gate, w_ple, g_ple_post, g_final):
    depth = g_mix.shape[0]
    assert depth == 1, "single-layer trunk"
    nb, seq, d = x_prompt.shape
    sb, sl, _ = x_sample.shape
    t_p, t_s = nb * seq, sb * sl
    t = t_p + t_s
    row = lambda v: v.reshape(1, -1)

    mixer_wts = (
        row(g_mix[0]), w_in[0].astype(BF16), conv_w[0], row(conv_b[0]),
        _block_diag_gates(w_rg_a[0], w_rg_x[0]), row(b_rg_a[0]), row(b_rg_x[0]), row(lru_lambda[0]),
        pool_w[0].astype(BF16), row(pool_scale[0]),
        w_proj_a[0].astype(BF16), w_proj_b[0].astype(BF16), w_out[0].astype(BF16),
    )

    tl = _pick_tile(256, seq)
    x1p, h_p, conv_p, pool_p = _mixer_prompt(x_prompt, mixer_wts, tl)
    x1s, h_s, conv_s, pool_s = _mixer_sample(
        x_sample.reshape(sb, sl * d), state_lru_h[0],
        state_conv[0].reshape(sb, (CONV_WIDTH - 1) * d), state_pool[0].reshape(sb, POOL_BUF * d),
        mixer_wts, sl)

    wr = jnp.pad(w_router[0], ((0, 0), (0, V7X_LANES - N_EXPERTS)))
    wr_hi = wr.astype(BF16)
    wr_lo = (wr - wr_hi.astype(F32)).astype(BF16)
    br = jnp.pad(b_router[0], (0, V7X_LANES - N_EXPERTS)).reshape(1, -1)
    tm = _pick_tile(512, t_p, t_s)
    idx, rank, wtok, cnt = _router(x1p, x1s, row(g_moe[0]), wr_hi, wr_lo, br, tm)

    tmx = _pick_tile(512, TOP_K * t)
    n_work = (TOP_K * t) // tmx + N_EXPERTS - 1
    pos, plan = _plan(idx, rank, cnt[:, 0], tmx, n_work)

    xs = _dispatch(pos, x1p, x1s, row(g_moe[0]), tm)
    ys = _experts(plan, xs, w_gate_up[0], b_gate_up[0][:, None, :], w_down[0], b_down[0][:, None, :], tmx)

    pp = p_prompt[0].reshape(t_p, -1)
    ps = jnp.swapaxes(p_sample[0], 0, 1).reshape(t_s, -1)
    tc = _pick_tile(256, t_p, t_s)
    consts = (row(g_ple[0]), w_ple_gate[0].astype(BF16), w_ple[0].astype(BF16), row(g_ple_post[0]), row(g_final))
    y_p, y_s = _combine(pos, ys, x1p, x1s, pp, ps, wtok, consts, tc)

    y_prompt = y_p.reshape(nb, seq, d)
    y_sample = jnp.swapaxes(y_s.reshape(sl, sb, d), 0, 1)
    return (y_prompt, y_sample,
            h_p.reshape(depth, nb, d), conv_p[None], pool_p[None],
            h_s[None], conv_s.reshape(depth, sb, CONV_WIDTH - 1, d), pool_s.reshape(depth, sb, POOL_BUF, d))
```

```python
import functools

import jax
import jax.numpy as jnp
from jax import lax
from jax.experimental import pallas as pl
from jax.experimental.pallas import tpu as pltpu

F32 = jnp.float32
BF16 = jnp.bfloat16
I32 = jnp.int32

EPS = 1e-6
LRU_C = 8.0
LRU_HEADS = 16
CONV_WIDTH = 4
POOL_WINDOWS = (2, 4, 8, 16)
POOL_BUF = max(POOL_WINDOWS) - 1
N_EXPERTS = 32
TOP_K = 4
SWIGLU_LIMIT = 7.0
SWIGLU_ALPHA = 1.702
PAST_LEN = 16384

V7X_LANES = 128
V7X_SUBLANES = 8
V7X_VMEM_BYTES = 64 * 1024 * 1024
VMEM_LIMIT = V7X_VMEM_BYTES - 8 * 1024 * 1024

EXPERT_SUB_ROWS = 128
EXPERT_HALF_ROWS = 512
HIDDEN_CHUNK = 512
ISSUE_UNROLL = 8
GATE_GROUP = 256
ROW_CHUNKS = 8


def _rmsnorm(x, g):
    ms = jnp.mean(x * x, axis=-1, keepdims=True)
    return x * lax.rsqrt(ms + EPS) * g


def _sigmoid(x):
    return 1.0 / (1.0 + jnp.exp(-x))


def _softplus(x):
    return jnp.maximum(x, 0.0) + jnp.log1p(jnp.exp(-jnp.abs(x)))


def _dot(a, b):
    return jnp.dot(a, b, preferred_element_type=F32)


def _in_proj(x, g, w_in):
    return _dot(_rmsnorm(x, g).astype(BF16), w_in)


def _gates(xc, wbd_ref, bra, brx):
    xcb = xc.astype(BF16)
    n_groups = xc.shape[1] // GATE_GROUP
    rs, gs = [], []
    for g in range(n_groups):
        o = _dot(xcb[:, g * GATE_GROUP:(g + 1) * GATE_GROUP], wbd_ref[g])
        rs.append(o[:, :GATE_GROUP])
        gs.append(o[:, GATE_GROUP:])
    r = _sigmoid(jnp.concatenate(rs, axis=-1) + bra)
    ig = _sigmoid(jnp.concatenate(gs, axis=-1) + brx)
    return r, ig


def _lru_coeffs(r, lam):
    log_a = (-LRU_C * _softplus(-lam)) * r
    a = jnp.exp(log_a)
    th = jnp.tanh(log_a)
    mult = jnp.sqrt(-2.0 * th / (1.0 - th))
    return a, mult


def _scan_rows(a, b, h0):
    tl, w = a.shape
    groups = tl // V7X_SUBLANES
    a3 = a.reshape(groups, V7X_SUBLANES, w)
    b3 = b.reshape(groups, V7X_SUBLANES, w)
    sub = lax.broadcasted_iota(I32, (groups, V7X_SUBLANES, w), 1)
    for s in (1, 2, 4):
        a_sh = pltpu.roll(a3, s, 1)
        b_sh = pltpu.roll(b3, s, 1)
        valid = sub >= s
        b3 = jnp.where(valid, a3 * b_sh + b3, b3)
        a3 = jnp.where(valid, a3 * a_sh, a3)
    hs = []
    h = h0
    for g in range(groups):
        hg = a3[g] * h + b3[g]
        hs.append(hg)
        h = hg[V7X_SUBLANES - 1:V7X_SUBLANES, :]
    return jnp.concatenate(hs, axis=0), h


def _merge_out(x, h, yb, ga, gb, wpa, wpb, wout):
    pa = _dot(h.astype(BF16), wpa)
    pb = _dot(yb.astype(BF16), wpb)
    merged = _sigmoid(ga) * pa + _sigmoid(gb) * pb
    return x + _dot(merged.astype(BF16), wout)


def _pool_proj(mixed, poolw_ref, pscale):
    gd = mixed.shape[1] // len(POOL_WINDOWS)
    mb = mixed.astype(BF16)
    outs = [_dot(mb[:, g * gd:(g + 1) * gd], poolw_ref[g]) for g in range(len(POOL_WINDOWS))]
    return jnp.concatenate(outs, axis=-1) * pscale


def _mixer_prompt_kernel(x_ref, gmix_ref, win_ref, convw_ref, convb_ref, wbd_ref, bra_ref, brx_ref, lam_ref,
                         poolw_ref, pscale_ref, wpa_ref, wpb_ref, wout_ref,
                         x1_ref, h_ref, conv_ref, pool_ref,
                         hc_ref, cc_ref, pc_ref, *, tl, n_tiles):
    j = pl.program_id(1)
    d = x_ref.shape[-1]

    @pl.when(j == 0)
    def _():
        hc_ref[...] = jnp.zeros_like(hc_ref)
        cc_ref[...] = jnp.zeros_like(cc_ref)
        pc_ref[...] = jnp.zeros_like(pc_ref)

    x = x_ref[...]
    z = _in_proj(x, gmix_ref[...], win_ref[...])
    xa, xb, ga, gb = z[:, :d], z[:, d:2 * d], z[:, 2 * d:3 * d], z[:, 3 * d:]
    row = lax.broadcasted_iota(I32, (tl, 1), 0)

    full = jnp.concatenate([cc_ref[...], xa], axis=0)
    cw = convw_ref[...]
    xc = convb_ref[...]
    for k in range(CONV_WIDTH):
        s = CONV_WIDTH - 1 - k
        term = xa if s == 0 else pltpu.roll(full, s, 0)[V7X_SUBLANES:]
        xc = xc + term * cw[k:k + 1]

    r, ig = _gates(xc, wbd_ref, bra_ref[...], brx_ref[...])
    a, mult = _lru_coeffs(r, lam_ref[...])
    mult = jnp.where(jnp.logical_and(j == 0, row == 0), 1.0, mult)
    bterm = mult * ig * xc
    h, h_last = _scan_rows(a, bterm, hc_ref[0:1, :])

    gd = d // len(POOL_WINDOWS)
    ext = jnp.concatenate([pc_ref[...], xb], axis=0)
    pad = pc_ref.shape[0]
    s2 = ext + pltpu.roll(ext, 1, 0)
    s4 = s2[:, gd:] + pltpu.roll(s2[:, gd:], 2, 0)
    s8 = s4[:, gd:] + pltpu.roll(s4[:, gd:], 4, 0)
    s16 = s8[:, gd:] + pltpu.roll(s8[:, gd:], 8, 0)
    tots = [s2[pad:, :gd], s4[pad:, :gd], s8[pad:, :gd], s16[pad:, :]]
    pos = j * tl + row
    mixed = []
    for g, win in enumerate(POOL_WINDOWS):
        inv = 1.0 / jnp.minimum(win, pos + 1).astype(F32)
        mixed.append(tots[g] * inv - xb[:, g * gd:(g + 1) * gd])
    yb = _pool_proj(jnp.concatenate(mixed, axis=-1), poolw_ref, pscale_ref[...])

    x1_ref[...] = _merge_out(x, h, yb, ga, gb, wpa_ref[...], wpb_ref[...], wout_ref[...])

    hc_ref[...] = jnp.broadcast_to(h_last, hc_ref.shape)
    cc_ref[...] = xa[tl - cc_ref.shape[0]:, :]
    pc_ref[...] = xb[tl - pc_ref.shape[0]:, :]

    @pl.when(j == n_tiles - 1)
    def _():
        h_ref[0] = h_last
        conv_ref[0] = xa[tl - (CONV_WIDTH - 1):, :]
        pool_ref[0] = xb[tl - POOL_BUF:, :]


def _const_spec(shape):
    nd = len(shape)
    return pl.BlockSpec(shape, lambda *_: (0,) * nd, pipeline_mode=pl.Buffered(1))


def _mixer_weight_specs(wts):
    return [_const_spec(w.shape) for w in wts]


def _mixer_prompt(x, wts, tl):
    nb, seq, d = x.shape
    n_tiles = seq // tl
    kern = functools.partial(_mixer_prompt_kernel, tl=tl, n_tiles=n_tiles)
    out_shape = (
        jax.ShapeDtypeStruct((nb * seq, d), F32),
        jax.ShapeDtypeStruct((nb, 1, d), F32),
        jax.ShapeDtypeStruct((nb, CONV_WIDTH - 1, d), F32),
        jax.ShapeDtypeStruct((nb, POOL_BUF, d), F32),
    )
    return pl.pallas_call(
        kern,
        grid=(nb, n_tiles),
        in_specs=[pl.BlockSpec((None, tl, d), lambda b, j: (b, j, 0))] + _mixer_weight_specs(wts),
        out_specs=(
            pl.BlockSpec((tl, d), lambda b, j: (b * n_tiles + j, 0)),
            pl.BlockSpec((1, 1, d), lambda b, j: (b, 0, 0)),
            pl.BlockSpec((1, CONV_WIDTH - 1, d), lambda b, j: (b, 0, 0)),
            pl.BlockSpec((1, POOL_BUF, d), lambda b, j: (b, 0, 0)),
        ),
        out_shape=out_shape,
        scratch_shapes=[
            pltpu.VMEM((V7X_SUBLANES, d), F32),
            pltpu.VMEM((V7X_SUBLANES, d), F32),
            pltpu.VMEM((2 * V7X_SUBLANES, d), F32),
        ],
        compiler_params=pltpu.CompilerParams(
            dimension_semantics=("arbitrary", "arbitrary"), vmem_limit_bytes=VMEM_LIMIT),
        name="mixer_prompt",
    )(x, *wts)


def _mixer_sample_kernel(x_ref, h0_ref, cbuf_ref, pbuf_ref,
                         gmix_ref, win_ref, convw_ref, convb_ref, wbd_ref, bra_ref, brx_ref, lam_ref,
                         poolw_ref, pscale_ref, wpa_ref, wpb_ref, wout_ref,
                         x1_ref, h_ref, conv_ref, pool_ref, *, sl):
    sb, d = h0_ref.shape
    xs = [x_ref[:, l * d:(l + 1) * d] for l in range(sl)]
    x = jnp.concatenate(xs, axis=0)
    z = _in_proj(x, gmix_ref[...], win_ref[...])
    rows = lambda v, l: v[l * sb:(l + 1) * sb]
    xa = [rows(z[:, :d], l) for l in range(sl)]
    xb = [rows(z[:, d:2 * d], l) for l in range(sl)]
    ga, gb = z[:, 2 * d:3 * d], z[:, 3 * d:]

    cext = [cbuf_ref[:, k * d:(k + 1) * d] for k in range(CONV_WIDTH - 1)] + xa
    cw = convw_ref[...]
    xcs = []
    for l in range(sl):
        acc = convb_ref[...]
        for k in range(CONV_WIDTH):
            acc = acc + cext[l + k] * cw[k:k + 1]
        xcs.append(acc)
    xc = jnp.concatenate(xcs, axis=0)
    for k in range(CONV_WIDTH - 1):
        conv_ref[:, k * d:(k + 1) * d] = cext[sl + k]

    r, ig = _gates(xc, wbd_ref, bra_ref[...], brx_ref[...])
    a, mult = _lru_coeffs(r, lam_ref[...])
    bterm = mult * ig * xc
    h = h0_ref[...]
    hs = []
    for l in range(sl):
        bl = rows(bterm, l)
        if PAST_LEN + l == 0:
            bl = rows(ig * xc, l)
        h = rows(a, l) * h + bl
        hs.append(h)
    h_ref[...] = h

    gd = d // len(POOL_WINDOWS)
    pext = [pbuf_ref[:, k * d:(k + 1) * d] for k in range(POOL_BUF)] + xb
    for k in range(POOL_BUF):
        pool_ref[:, k * d:(k + 1) * d] = pext[sl + k]
    mixed_rows = []
    for l in range(sl):
        parts = []
        for g, win in enumerate(POOL_WINDOWS):
            sl_g = slice(g * gd, (g + 1) * gd)
            tot = pext[POOL_BUF + l][:, sl_g]
            for jj in range(1, win):
                tot = tot + pext[POOL_BUF + l - jj][:, sl_g]
            cnt = float(min(win, PAST_LEN + l + 1))
            parts.append(tot / cnt - xb[l][:, sl_g])
        mixed_rows.append(jnp.concatenate(parts, axis=-1))
    yb = _pool_proj(jnp.concatenate(mixed_rows, axis=0), poolw_ref, pscale_ref[...])

    x1_ref[...] = _merge_out(x, jnp.concatenate(hs, axis=0), yb, ga, gb,
                             wpa_ref[...], wpb_ref[...], wout_ref[...])


def _mixer_sample(x2d, h0, cbuf2d, pbuf2d, wts, sl):
    sb, d = h0.shape
    kern = functools.partial(_mixer_sample_kernel, sl=sl)
    ins = (x2d, h0, cbuf2d, pbuf2d) + tuple(wts)
    out_shape = (
        jax.ShapeDtypeStruct((sl * sb, d), F32),
        jax.ShapeDtypeStruct((sb, d), F32),
        jax.ShapeDtypeStruct(cbuf2d.shape, F32),
        jax.ShapeDtypeStruct(pbuf2d.shape, F32),
    )
    full = lambda s: pl.BlockSpec(s, lambda i: (0,) * len(s))
    return pl.pallas_call(
        kern,
        grid=(1,),
        in_specs=[full(v.shape) for v in ins],
        out_specs=tuple(full(s.shape) for s in out_shape),
        out_shape=out_shape,
        compiler_params=pltpu.CompilerParams(
            dimension_semantics=("arbitrary",), vmem_limit_bytes=VMEM_LIMIT),
        name="mixer_sample",
    )(*ins)


def _two_source_specs(tm, width, n_p_tiles):
    return [
        pl.BlockSpec((tm, width), lambda i, *_: (jnp.minimum(i, n_p_tiles - 1), 0)),
        pl.BlockSpec((tm, width), lambda i, *_: (jnp.maximum(i - n_p_tiles, 0), 0)),
    ]


def _router_kernel(xp_ref, xs_ref, g_ref, whi_ref, wlo_ref, br_ref, tri_ref,
                   idx_ref, rank_ref, wtok_ref, cnt_ref, base_ref, *, n_p_tiles):
    i = pl.program_id(0)
    tm = xp_ref.shape[0]

    @pl.when(i == 0)
    def _():
        base_ref[...] = jnp.zeros_like(base_ref)

    x = jnp.where(i < n_p_tiles, xp_ref[...], xs_ref[...])
    u = _rmsnorm(x, g_ref[...])
    u_hi = u.astype(BF16)
    u_lo = (u - u_hi.astype(F32)).astype(BF16)
    logits = _dot(u_hi, whi_ref[...]) + (_dot(u_hi, wlo_ref[...]) + _dot(u_lo, whi_ref[...]))
    lt = (logits + br_ref[...]).T[:N_EXPERTS]

    eio = lax.broadcasted_iota(I32, (N_EXPERTS, tm), 0)
    vals, idxs, sels = [], [], []
    cur = lt
    for _ in range(TOP_K):
        m = jnp.max(cur, axis=0, keepdims=True)
        ik = jnp.min(jnp.where(cur == m, eio, N_EXPERTS), axis=0, keepdims=True)
        sel = eio == ik
        vals.append(m)
        idxs.append(ik)
        sels.append(sel)
        cur = jnp.where(sel, -jnp.inf, cur)
    es = [jnp.exp(v - vals[0]) for v in vals]
    den = es[0]
    for e in es[1:]:
        den = den + e
    ws = [e / den for e in es]

    multi = sels[0].astype(F32)
    for s in sels[1:]:
        multi = multi + s.astype(F32)
    before = _dot(multi.astype(BF16), tri_ref[...]) + base_ref[:, 0:1]
    ranks = [jnp.sum(jnp.where(s, before, 0.0), axis=0, keepdims=True).astype(I32) for s in sels]

    idx_ref[...] = jnp.concatenate(idxs, axis=0)
    rank_ref[...] = jnp.concatenate(ranks, axis=0)
    wpad = jnp.concatenate(ws + [jnp.zeros((V7X_LANES - TOP_K, tm), F32)], axis=0)
    wtok_ref[...] = wpad.T
    new_base = base_ref[...] + jnp.sum(multi, axis=1, keepdims=True)
    base_ref[...] = new_base
    cnt_ref[...] = new_base.astype(I32)


def _router(x1p, x1s, g, whi, wlo, br, tm):
    t_p, d = x1p.shape
    t_s = x1s.shape[0]
    n_p_tiles, n_s_tiles = t_p // tm, t_s // tm
    t = t_p + t_s
    tri = (lax.broadcasted_iota(I32, (tm, tm), 0) < lax.broadcasted_iota(I32, (tm, tm), 1)).astype(BF16)
    kern = functools.partial(_router_kernel, n_p_tiles=n_p_tiles)
    out_shape = (
        jax.ShapeDtypeStruct((TOP_K, t), I32),
        jax.ShapeDtypeStruct((TOP_K, t), I32),
        jax.ShapeDtypeStruct((t, V7X_LANES), F32),
        jax.ShapeDtypeStruct((N_EXPERTS, V7X_LANES), I32),
    )
    consts = (g, whi, wlo, br, tri)
    return pl.pallas_call(
        kern,
        grid=(n_p_tiles + n_s_tiles,),
        in_specs=_two_source_specs(tm, d, n_p_tiles) + [_const_spec(c.shape) for c in consts],
        out_specs=(
            pl.BlockSpec((TOP_K, tm), lambda i: (0, i)),
            pl.BlockSpec((TOP_K, tm), lambda i: (0, i)),
            pl.BlockSpec((tm, V7X_LANES), lambda i: (i, 0)),
            pl.BlockSpec((N_EXPERTS, V7X_LANES), lambda i: (0, 0)),
        ),
        out_shape=out_shape,
        scratch_shapes=[pltpu.VMEM((N_EXPERTS, V7X_LANES), F32)],
        compiler_params=pltpu.CompilerParams(
            dimension_semantics=("arbitrary",), vmem_limit_bytes=VMEM_LIMIT),
        name="router",
    )(x1p, x1s, *consts)


def _row_copy(src, src_row, dst, dst_row, sem):
    return pltpu.make_async_copy(
        src.at[pl.ds(pl.multiple_of(src_row * ROW_CHUNKS, ROW_CHUNKS), ROW_CHUNKS), :],
        dst.at[pl.ds(pl.multiple_of(dst_row * ROW_CHUNKS, ROW_CHUNKS), ROW_CHUNKS), :],
        sem)


def _to_row_tiles(dst_ref, base, val):
    n = val.shape[0]
    for c in range(ROW_CHUNKS):
        dst_ref[pl.ds(base + c, n, stride=ROW_CHUNKS), :] = val[:, c * V7X_LANES:(c + 1) * V7X_LANES]


def _to_row_tiles_range(dst_ref, base, val, tmp_ref, first_row, lo, hi):
    n = val.shape[0]
    _to_row_tiles(tmp_ref, 0, val)
    tok = first_row + lax.shift_right_logical(
        lax.broadcasted_iota(I32, (n * ROW_CHUNKS, V7X_LANES), 0), ROW_CHUNKS.bit_length() - 1)
    mask = jnp.logical_and(tok >= lo, tok < hi)
    pltpu.store(dst_ref.at[pl.ds(base, n * ROW_CHUNKS), :], tmp_ref[...], mask=mask)


def _from_row_tiles(src_ref, base, n):
    return jnp.concatenate(
        [src_ref[pl.ds(base + c, n, stride=ROW_CHUNKS), :] for c in range(ROW_CHUNKS)], axis=-1)


def _issue_rows(n_tokens, issue_token):
    def trip(g, carry):
        for u in range(ISSUE_UNROLL):
            issue_token(g * ISSUE_UNROLL + u)
        return carry

    lax.fori_loop(0, n_tokens // ISSUE_UNROLL, trip, 0)


def _dispatch_kernel(pos_ref, xp_ref, xs_ref, g_ref, out_hbm, slab_ref, sems, *, n_p_tiles, n_tiles):
    i = pl.program_id(0)
    td = xp_ref.shape[0]
    slot = i % 2
    x = jnp.where(i < n_p_tiles, xp_ref[...], xs_ref[...])
    _to_row_tiles(slab_ref, slot * (td * ROW_CHUNKS), _rmsnorm(x, g_ref[...]))

    def issue_token(t):
        for k in range(TOP_K):
            _row_copy(slab_ref, slot * td + t, out_hbm, pos_ref[k, t], sems.at[slot]).start(priority=k % 2)

    _issue_rows(td, issue_token)

    def drain(s):
        for _ in range(TOP_K):
            pltpu.make_async_copy(slab_ref.at[pl.ds(0, td * ROW_CHUNKS), :],
                                  out_hbm.at[pl.ds(0, td * ROW_CHUNKS), :], sems.at[s]).wait()

    @pl.when(i > 0)
    def _():
        drain(1 - slot)

    @pl.when(i == n_tiles - 1)
    def _():
        drain(slot)


def _dispatch(pos, x1p, x1s, g, td):
    t_p, d = x1p.shape
    t_s = x1s.shape[0]
    n_p_tiles, n_s_tiles = t_p // td, t_s // td
    t = t_p + t_s
    kern = functools.partial(_dispatch_kernel, n_p_tiles=n_p_tiles, n_tiles=n_p_tiles + n_s_tiles)
    return pl.pallas_call(
        kern,
        grid=(n_p_tiles + n_s_tiles,),
        in_specs=[pl.BlockSpec((TOP_K, td), lambda i: (0, i), memory_space=pltpu.SMEM)]
        + _two_source_specs(td, d, n_p_tiles) + [_const_spec(g.shape)],
        out_specs=pl.BlockSpec(memory_space=pl.ANY),
        out_shape=jax.ShapeDtypeStruct((TOP_K * t * ROW_CHUNKS, V7X_LANES), F32),
        scratch_shapes=[pltpu.VMEM((2 * td * ROW_CHUNKS, V7X_LANES), F32), pltpu.SemaphoreType.DMA((2,))],
        compiler_params=pltpu.CompilerParams(
            dimension_semantics=("arbitrary",), vmem_limit_bytes=VMEM_LIMIT),
        name="dispatch",
    )(pos, x1p, x1s, g)


def _experts_kernel(tile_ref, exp_ref, lo_ref, hi_ref, wchg_ref,
                    xs_ref, wgu_ref, bgu_ref, wd_ref, bd_ref, ys_ref, wgu_s, wd_s, tmp_ref, *, tmx, sub):
    i = pl.program_id(0)
    de = wd_ref.shape[1]

    @pl.when(wchg_ref[i] == 1)
    def _():
        wgu_s[...] = wgu_ref[0].astype(BF16)
        wd_s[...] = wd_ref[0].astype(BF16)

    def mlp(first_row, n):
        x = _from_row_tiles(xs_ref, first_row * ROW_CHUNKS, n).astype(BF16)
        y = None
        for c0 in range(0, de, HIDDEN_CHUNK):
            c1 = c0 + HIDDEN_CHUNK
            gate = _dot(x, wgu_s[:, c0:c1]) + bgu_ref[0, :, c0:c1]
            up = _dot(x, wgu_s[:, de + c0:de + c1]) + bgu_ref[0, :, de + c0:de + c1]
            gate = jnp.minimum(gate, SWIGLU_LIMIT)
            up = jnp.clip(up, -SWIGLU_LIMIT, SWIGLU_LIMIT)
            h = (up + 1.0) * (gate * _sigmoid(SWIGLU_ALPHA * gate))
            part = _dot(h.astype(BF16), wd_s[c0:c1, :])
            y = part if y is None else y + part
        return y + bd_ref[0]

    lo, hi = lo_ref[i], hi_ref[i]
    t0 = tile_ref[i] * tmx
    owns_tile = jnp.logical_and(lo <= t0, hi >= t0 + tmx)

    @pl.when(owns_tile)
    def _():
        for r in range(0, tmx, EXPERT_HALF_ROWS):
            _to_row_tiles(ys_ref, r * ROW_CHUNKS, mlp(r, EXPERT_HALF_ROWS))

    for sb in range(tmx // sub):
        r0 = t0 + sb * sub
        overlaps = jnp.logical_and(hi > r0, lo < r0 + sub)

        @pl.when(jnp.logical_and(overlaps, jnp.logical_not(owns_tile)))
        def _(sb=sb, r0=r0):
            y = mlp(sb * sub, sub)
            whole = jnp.logical_and(lo <= r0, hi >= r0 + sub)

            @pl.when(whole)
            def _():
                _to_row_tiles(ys_ref, sb * sub * ROW_CHUNKS, y)

            @pl.when(jnp.logical_not(whole))
            def _():
                _to_row_tiles_range(ys_ref, sb * sub * ROW_CHUNKS, y, tmp_ref, r0, lo, hi)


def _experts(plan, xs, wgu, bgu, wd, bd, tmx, sub):
    n_rows = xs.shape[0] // ROW_CHUNKS
    n_work = plan[0].shape[0]
    _, d, de2 = wgu.shape
    de = wd.shape[1]
    kern = functools.partial(_experts_kernel, tmx=tmx, sub=sub)
    grid_spec = pltpu.PrefetchScalarGridSpec(
        num_scalar_prefetch=len(plan),
        grid=(n_work,),
        in_specs=[
            pl.BlockSpec((tmx * ROW_CHUNKS, V7X_LANES), lambda i, tile, *_: (tile[i], 0)),
            pl.BlockSpec((1, d, de2), lambda i, tile, ex, *_: (ex[i], 0, 0)),
            pl.BlockSpec((1, 1, de2), lambda i, tile, ex, *_: (ex[i], 0, 0)),
            pl.BlockSpec((1, de, d), lambda i, tile, ex, *_: (ex[i], 0, 0)),
            pl.BlockSpec((1, 1, d), lambda i, tile, ex, *_: (ex[i], 0, 0)),
        ],
        out_specs=pl.BlockSpec((tmx * ROW_CHUNKS, V7X_LANES), lambda i, tile, *_: (tile[i], 0)),
        scratch_shapes=[pltpu.VMEM((d, de2), BF16), pltpu.VMEM((de, d), BF16),
                        pltpu.VMEM((sub * ROW_CHUNKS, V7X_LANES), F32)],
    )
    return pl.pallas_call(
        kern,
        grid_spec=grid_spec,
        out_shape=jax.ShapeDtypeStruct((n_rows * ROW_CHUNKS, V7X_LANES), F32),
        compiler_params=pltpu.CompilerParams(
            dimension_semantics=("arbitrary",), vmem_limit_bytes=VMEM_LIMIT),
        name="experts",
    )(*plan, xs, wgu, bgu, wd, bd)


def _combine_kernel(pos_ref, posn_ref, ys_hbm, xp_ref, xs_ref, pp_ref, ps_ref, wtok_ref,
                    gple_ref, wgate_ref, wple_ref, gpost_ref, gfin_ref,
                    yp_ref, ysm_ref, gath_ref, sems, *, n_p_tiles, n_tiles):
    i = pl.program_id(0)
    tc = xp_ref.shape[0]
    slot = i % 2
    slot_rows = TOP_K * tc

    def gather(p_ref, s):
        def issue_token(t):
            for k in range(TOP_K):
                _row_copy(ys_hbm, p_ref[k, t], gath_ref, s * slot_rows + k * tc + t,
                          sems.at[s]).start(priority=k % 2)

        _issue_rows(tc, issue_token)

    @pl.when(i == 0)
    def _():
        gather(pos_ref, slot)

    @pl.when(i + 1 < n_tiles)
    def _():
        gather(posn_ref, 1 - slot)

    is_p = i < n_p_tiles
    x1 = jnp.where(is_p, xp_ref[...], xs_ref[...])
    p = jnp.where(is_p, pp_ref[...], ps_ref[...])
    ple = _rmsnorm(_dot(p.astype(BF16), wple_ref[...]), gpost_ref[...])
    base = slot * (slot_rows * ROW_CHUNKS)
    pltpu.make_async_copy(ys_hbm.at[pl.ds(0, slot_rows * ROW_CHUNKS), :],
                          gath_ref.at[pl.ds(base, slot_rows * ROW_CHUNKS), :], sems.at[slot]).wait()

    wt = wtok_ref[...]
    moe = wt[:, 0:1] * _from_row_tiles(gath_ref, base, tc)
    for k in range(1, TOP_K):
        moe = moe + wt[:, k:k + 1] * _from_row_tiles(gath_ref, base + k * tc * ROW_CHUNKS, tc)
    x2 = x1 + moe
    gate = _sigmoid(_dot(_rmsnorm(x2, gple_ref[...]).astype(BF16), wgate_ref[...]))
    y = _rmsnorm(x2 + ple * gate, gfin_ref[...])

    @pl.when(is_p)
    def _():
        yp_ref[...] = y

    @pl.when(jnp.logical_not(is_p))
    def _():
        ysm_ref[...] = y


def _combine(pos, ys, x1p, x1s, pp, ps, wtok, consts, tc):
    t_p, d = x1p.shape
    t_s = x1s.shape[0]
    n_p_tiles, n_s_tiles = t_p // tc, t_s // tc
    n_tiles = n_p_tiles + n_s_tiles
    kern = functools.partial(_combine_kernel, n_p_tiles=n_p_tiles, n_tiles=n_tiles)
    return pl.pallas_call(
        kern,
        grid=(n_tiles,),
        in_specs=[pl.BlockSpec((TOP_K, tc), lambda i: (0, i), memory_space=pltpu.SMEM),
                  pl.BlockSpec((TOP_K, tc), lambda i: (0, jnp.minimum(i + 1, n_tiles - 1)),
                               memory_space=pltpu.SMEM),
                  pl.BlockSpec(memory_space=pl.ANY)]
        + _two_source_specs(tc, d, n_p_tiles) + _two_source_specs(tc, pp.shape[1], n_p_tiles)
        + [pl.BlockSpec((tc, V7X_LANES), lambda i: (i, 0))] + [_const_spec(c.shape) for c in consts],
        out_specs=tuple(_two_source_specs(tc, d, n_p_tiles)),
        out_shape=(jax.ShapeDtypeStruct((t_p, d), F32), jax.ShapeDtypeStruct((t_s, d), F32)),
        scratch_shapes=[pltpu.VMEM((2 * TOP_K * tc * ROW_CHUNKS, V7X_LANES), F32),
                        pltpu.SemaphoreType.DMA((2,))],
        compiler_params=pltpu.CompilerParams(
            dimension_semantics=("arbitrary",), vmem_limit_bytes=VMEM_LIMIT),
        name="combine",
    )(pos, pos, ys, x1p, x1s, pp, ps, wtok, *consts)


def _plan(idx, rank, counts, tmx, n_work):
    eids = jnp.arange(N_EXPERTS, dtype=I32)
    incl = eids[None, :] <= eids[:, None]
    ends = jnp.sum(jnp.where(incl, counts[None, :], 0), axis=1)
    offs = ends - counts
    pos = rank + jnp.sum(jnp.where(idx[..., None] == eids, offs, 0), axis=-1)

    first_tile = offs // tmx
    last_tile = (ends - 1) // tmx
    n_e = jnp.where(counts > 0, last_tile - first_tile + 1, 0)
    iend = jnp.sum(jnp.where(incl, n_e[None, :], 0), axis=1)
    istart = iend - n_e
    total = iend[N_EXPERTS - 1]
    i = jnp.arange(n_work, dtype=I32)
    ic = jnp.minimum(i, total - 1)
    e_i = jnp.sum((ic[:, None] >= iend[None, :]).astype(I32), axis=1)
    onehot = e_i[:, None] == eids[None, :]
    pick = lambda v: jnp.sum(jnp.where(onehot, v[None, :], 0), axis=1)
    tile_i = pick(first_tile) + (ic - pick(istart))
    valid = i < total
    lo = jnp.where(valid, jnp.maximum(pick(offs), tile_i * tmx), 0)
    hi = jnp.where(valid, jnp.minimum(pick(ends), (tile_i + 1) * tmx), 0)
    prev_e = jnp.concatenate([jnp.full((1,), -1, I32), e_i[:-1]])
    wchg = (e_i != prev_e).astype(I32)
    return pos.astype(I32), tuple(v.astype(I32) for v in (tile_i, e_i, lo, hi, wchg))


def _block_diag_gates(w_a, w_x):
    heads, hd, _ = w_a.shape
    per = GATE_GROUP // hd
    groups = heads // per
    eye = jnp.eye(per, dtype=w_a.dtype)

    def bd(w):
        w4 = w.reshape(groups, per, hd, hd)
        return jnp.einsum('ghij,hk->ghikj', w4, eye).reshape(groups, GATE_GROUP, GATE_GROUP)

    return jnp.concatenate([bd(w_a), bd(w_x)], axis=-1).astype(BF16)


def _pick_tile(pref, *sizes):
    t = pref
    while any(s % t for s in sizes):
        t //= 2
    return t


def kernel(x_prompt, x_sample, p_prompt, p_sample, state_lru_h, state_conv, state_pool, g_mix, w_in, conv_w, conv_b, w_rg_a, b_rg_a, w_rg_x, b_rg_x, lru_lambda, pool_w, pool_scale, w_proj_a, w_proj_b, w_out, g_moe, w_router, b_router, w_gate_up, b_gate_up, w_down, b_down, g_ple, w_ple_gate, w_ple, g_ple_post, g_final):
    depth = g_mix.shape[0]
    assert depth == 1, "single-layer trunk"
    nb, seq, d = x_prompt.shape
    sb, sl, _ = x_sample.shape
    t_p, t_s = nb * seq, sb * sl
    t = t_p + t_s
    row = lambda v: v.reshape(1, -1)

    mixer_wts = (
        row(g_mix[0]), w_in[0].astype(BF16), conv_w[0], row(conv_b[0]),
        _block_diag_gates(w_rg_a[0], w_rg_x[0]), row(b_rg_a[0]), row(b_rg_x[0]), row(lru_lambda[0]),
        pool_w[0].astype(BF16), row(pool_scale[0]),
        w_proj_a[0].astype(BF16), w_proj_b[0].astype(BF16), w_out[0].astype(BF16),
    )

    tl = _pick_tile(256, seq)
    x1p, h_p, conv_p, pool_p = _mixer_prompt(x_prompt, mixer_wts, tl)
    x1s, h_s, conv_s, pool_s = _mixer_sample(
        x_sample.reshape(sb, sl * d), state_lru_h[0],
        state_conv[0].reshape(sb, (CONV_WIDTH - 1) * d), state_pool[0].reshape(sb, POOL_BUF * d),
        mixer_wts, sl)

    wr = jnp.pad(w_router[0], ((0, 0), (0, V7X_LANES - N_EXPERTS)))
    wr_hi = wr.astype(BF16)
    wr_lo = (wr - wr_hi.astype(F32)).astype(BF16)
    br = jnp.pad(b_router[0], (0, V7X_LANES - N_EXPERTS)).reshape(1, -1)
    tm = _pick_tile(512, t_p, t_s)
    idx, rank, wtok, cnt = _router(x1p, x1s, row(g_moe[0]), wr_hi, wr_lo, br, tm)

    tmx = _pick_tile(512, TOP_K * t)
    n_work = (TOP_K * t) // tmx + N_EXPERTS - 1
    pos, plan = _plan(idx, rank, cnt[:, 0], tmx, n_work)

    xs = _dispatch(pos, x1p, x1s, row(g_moe[0]), tm)
    ys = _experts(plan, xs, w_gate_up[0], b_gate_up[0][:, None, :], w_down[0], b_down[0][:, None, :], tmx,
                  _pick_tile(EXPERT_SUB_ROWS, tmx))

    pp = p_prompt[0].reshape(t_p, -1)
    ps = jnp.swapaxes(p_sample[0], 0, 1).reshape(t_s, -1)
    tc = _pick_tile(256, t_p, t_s)
    consts = (row(g_ple[0]), w_ple_gate[0].astype(BF16), w_ple[0].astype(BF16), row(g_ple_post[0]), row(g_final))
    y_p, y_s = _combine(pos, ys, x1p, x1s, pp, ps, wtok, consts, tc)

    y_prompt = y_p.reshape(nb, seq, d)
    y_sample = jnp.swapaxes(y_s.reshape(sl, sb, d), 0, 1)
    return (y_prompt, y_sample,
            h_p.reshape(depth, nb, d), conv_p[None], pool_p[None],
            h_s[None], conv_s.reshape(depth, sb, CONV_WIDTH - 1, d), pool_s.reshape(depth, sb, POOL_BUF, d))
```

```python
import functools

import jax
import jax.numpy as jnp
from jax import lax
from jax.experimental import pallas as pl
from jax.experimental.pallas import tpu as pltpu

F32 = jnp.float32
BF16 = jnp.bfloat16
I32 = jnp.int32

EPS = 1e-6
LRU_C = 8.0
LRU_HEADS = 16
CONV_WIDTH = 4
POOL_WINDOWS = (2, 4, 8, 16)
POOL_BUF = max(POOL_WINDOWS) - 1
N_EXPERTS = 32
TOP_K = 4
SWIGLU_LIMIT = 7.0
SWIGLU_ALPHA = 1.702
PAST_LEN = 16384

V7X_LANES = 128
V7X_SUBLANES = 8
V7X_VMEM_BYTES = 64 * 1024 * 1024
VMEM_LIMIT = V7X_VMEM_BYTES - 8 * 1024 * 1024

EXPERT_SUB_ROWS = 128
EXPERT_HALF_ROWS = 512
HIDDEN_CHUNK = 512
ISSUE_UNROLL = 8
MIXER_SEQS_PER_STEP = 2
MIXER_STAGE_LAG = 5
MIXER_SCHEDULER_FLAGS = None
GATE_GROUP = 256
ROW_CHUNKS = 8


def _rmsnorm(x, g):
    ms = jnp.mean(x * x, axis=-1, keepdims=True)
    return x * lax.rsqrt(ms + EPS) * g


def _sigmoid(x):
    return 1.0 / (1.0 + jnp.exp(-x))


def _softplus(x):
    return jnp.maximum(x, 0.0) + jnp.log1p(jnp.exp(-jnp.abs(x)))


def _dot(a, b):
    return jnp.dot(a, b, preferred_element_type=F32)


def _in_proj(x, g, w_in):
    return _dot(_rmsnorm(x, g).astype(BF16), w_in)


def _gates(xc, wbd_ref, bra, brx):
    xcb = xc.astype(BF16)
    n_groups = xc.shape[1] // GATE_GROUP
    rs, gs = [], []
    for g in range(n_groups):
        o = _dot(xcb[:, g * GATE_GROUP:(g + 1) * GATE_GROUP], wbd_ref[g])
        rs.append(o[:, :GATE_GROUP])
        gs.append(o[:, GATE_GROUP:])
    r = _sigmoid(jnp.concatenate(rs, axis=-1) + bra)
    ig = _sigmoid(jnp.concatenate(gs, axis=-1) + brx)
    return r, ig


def _lru_coeffs(r, lam):
    log_a = (-LRU_C * _softplus(-lam)) * r
    a = jnp.exp(log_a)
    th = jnp.tanh(log_a)
    q = -2.0 * th
    mult = jnp.where(q > 0.0, q * lax.rsqrt(q * (1.0 - th)), 0.0)
    return a, mult


def _scan_rows(a, b, h0):
    tl, w = a.shape
    groups = tl // V7X_SUBLANES
    a3 = a.reshape(groups, V7X_SUBLANES, w)
    b3 = b.reshape(groups, V7X_SUBLANES, w)
    sub = lax.broadcasted_iota(I32, (groups, V7X_SUBLANES, w), 1)
    for s in (1, 2, 4):
        a_sh = pltpu.roll(a3, s, 1)
        b_sh = pltpu.roll(b3, s, 1)
        valid = sub >= s
        b3 = jnp.where(valid, a3 * b_sh + b3, b3)
        a3 = jnp.where(valid, a3 * a_sh, a3)
    hs = []
    h = h0
    for g in range(groups):
        hg = a3[g] * h + b3[g]
        hs.append(hg)
        h = hg[V7X_SUBLANES - 1:V7X_SUBLANES, :]
    return jnp.concatenate(hs, axis=0), h


def _merge_out(x, h, yb, ga, gb, wpa, wpb, wout):
    pa = _dot(h.astype(BF16), wpa)
    pb = _dot(yb.astype(BF16), wpb)
    merged = _sigmoid(ga) * pa + _sigmoid(gb) * pb
    return x + _dot(merged.astype(BF16), wout)


def _pool_proj(mixed, poolw_ref, pscale):
    gd = mixed.shape[1] // len(POOL_WINDOWS)
    mb = mixed.astype(BF16)
    outs = [_dot(mb[:, g * gd:(g + 1) * gd], poolw_ref[g]) for g in range(len(POOL_WINDOWS))]
    return jnp.concatenate(outs, axis=-1) * pscale


def _mixer_prompt_kernel(x_ref, gmix_ref, win_ref, convw_ref, convb_ref, wbd_ref, bra_ref, brx_ref, lam_ref,
                         poolw_ref, pscale_ref, wpa_ref, wpb_ref, wout_ref,
                         x1_ref, h_ref, conv_ref, pool_ref,
                         hc_ref, cc_ref, pc_ref, *, tl, n_tiles):
    j = pl.program_id(1)
    d = x_ref.shape[-1]

    @pl.when(j == 0)
    def _():
        hc_ref[...] = jnp.zeros_like(hc_ref)
        cc_ref[...] = jnp.zeros_like(cc_ref)
        pc_ref[...] = jnp.zeros_like(pc_ref)

    tiles = [_mixer_prompt_tile(j, x_ref.at[q], gmix_ref, win_ref, convw_ref, convb_ref, wbd_ref, bra_ref,
                                brx_ref, lam_ref, poolw_ref, pscale_ref, wpa_ref, wpb_ref, wout_ref,
                                x1_ref.at[q], h_ref.at[q], conv_ref.at[q], pool_ref.at[q],
                                hc_ref.at[q], cc_ref.at[q], pc_ref.at[q], tl=tl, n_tiles=n_tiles)
             for q in range(x_ref.shape[0])]
    _interleave(tiles, MIXER_STAGE_LAG)


def _interleave(stage_iters, lag):
    live = dict(enumerate(stage_iters))
    step = 0
    while live:
        for q in sorted(live):
            if step >= q * lag:
                try:
                    next(live[q])
                except StopIteration:
                    del live[q]
        step += 1


def _mixer_prompt_tile(j, x_ref, gmix_ref, win_ref, convw_ref, convb_ref, wbd_ref, bra_ref, brx_ref, lam_ref,
                       poolw_ref, pscale_ref, wpa_ref, wpb_ref, wout_ref,
                       x1_ref, h_ref, conv_ref, pool_ref,
                       hc_ref, cc_ref, pc_ref, *, tl, n_tiles):
    d = x_ref.shape[-1]
    x = x_ref[...]
    u = _rmsnorm(x, gmix_ref[...]).astype(BF16)
    yield
    xa = _dot(u, win_ref[:, 0:d])
    yield
    xb = _dot(u, win_ref[:, d:2 * d])
    yield
    ga = _dot(u, win_ref[:, 2 * d:3 * d])
    yield
    gb = _dot(u, win_ref[:, 3 * d:4 * d])
    yield
    row = lax.broadcasted_iota(I32, (tl, 1), 0)

    full = jnp.concatenate([cc_ref[...], xa], axis=0)
    cw = convw_ref[...]
    xc = convb_ref[...]
    for k in range(CONV_WIDTH):
        s = CONV_WIDTH - 1 - k
        term = xa if s == 0 else pltpu.roll(full, s, 0)[V7X_SUBLANES:]
        xc = xc + term * cw[k:k + 1]
    yield

    r, ig = _gates(xc, wbd_ref, bra_ref[...], brx_ref[...])
    yield
    a, mult = _lru_coeffs(r, lam_ref[...])
    mult = jnp.where(jnp.logical_and(j == 0, row == 0), 1.0, mult)
    bterm = mult * ig * xc
    yield
    h, h_last = _scan_rows(a, bterm, hc_ref[0:1, :])
    yield

    gd = d // len(POOL_WINDOWS)
    ext = jnp.concatenate([pc_ref[...], xb], axis=0)
    pad = pc_ref.shape[0]
    s2 = ext + pltpu.roll(ext, 1, 0)
    s4 = s2[:, gd:] + pltpu.roll(s2[:, gd:], 2, 0)
    s8 = s4[:, gd:] + pltpu.roll(s4[:, gd:], 4, 0)
    s16 = s8[:, gd:] + pltpu.roll(s8[:, gd:], 8, 0)
    tots = [s2[pad:, :gd], s4[pad:, :gd], s8[pad:, :gd], s16[pad:, :]]
    pos = j * tl + row
    mixed = []
    for g, win in enumerate(POOL_WINDOWS):
        inv = 1.0 / jnp.minimum(win, pos + 1).astype(F32)
        mixed.append(tots[g] * inv - xb[:, g * gd:(g + 1) * gd])
    yield
    yb = _pool_proj(jnp.concatenate(mixed, axis=-1), poolw_ref, pscale_ref[...])
    sga = _sigmoid(ga)
    sgb = _sigmoid(gb)
    yield
    pa = _dot(h.astype(BF16), wpa_ref[...])
    yield
    pb = _dot(yb.astype(BF16), wpb_ref[...])
    merged = (sga * pa + sgb * pb).astype(BF16)
    yield
    x1_ref[...] = x + _dot(merged, wout_ref[...])

    hc_ref[...] = jnp.broadcast_to(h_last, hc_ref.shape)
    cc_ref[...] = xa[tl - cc_ref.shape[0]:, :]
    pc_ref[...] = xb[tl - pc_ref.shape[0]:, :]

    @pl.when(j == n_tiles - 1)
    def _():
        h_ref[...] = h_last
        conv_ref[...] = xa[tl - (CONV_WIDTH - 1):, :]
        pool_ref[...] = xb[tl - POOL_BUF:, :]


def _const_spec(shape):
    nd = len(shape)
    return pl.BlockSpec(shape, lambda *_: (0,) * nd, pipeline_mode=pl.Buffered(1))


def _mixer_weight_specs(wts):
    return [_const_spec(w.shape) for w in wts]


def _mixer_prompt(x, wts, tl):
    nb, seq, d = x.shape
    n_tiles = seq // tl
    nq = _pick_tile(MIXER_SEQS_PER_STEP, nb)
    kern = functools.partial(_mixer_prompt_kernel, tl=tl, n_tiles=n_tiles)
    out_shape = (
        jax.ShapeDtypeStruct((nb, seq, d), F32),
        jax.ShapeDtypeStruct((nb, 1, d), F32),
        jax.ShapeDtypeStruct((nb, CONV_WIDTH - 1, d), F32),
        jax.ShapeDtypeStruct((nb, POOL_BUF, d), F32),
    )
    return pl.pallas_call(
        kern,
        grid=(nb // nq, n_tiles),
        in_specs=[pl.BlockSpec((nq, tl, d), lambda b, j: (b, j, 0))] + _mixer_weight_specs(wts),
        out_specs=(
            pl.BlockSpec((nq, tl, d), lambda b, j: (b, j, 0)),
            pl.BlockSpec((nq, 1, d), lambda b, j: (b, 0, 0)),
            pl.BlockSpec((nq, CONV_WIDTH - 1, d), lambda b, j: (b, 0, 0)),
            pl.BlockSpec((nq, POOL_BUF, d), lambda b, j: (b, 0, 0)),
        ),
        out_shape=out_shape,
        scratch_shapes=[
            pltpu.VMEM((nq, V7X_SUBLANES, d), F32),
            pltpu.VMEM((nq, V7X_SUBLANES, d), F32),
            pltpu.VMEM((nq, 2 * V7X_SUBLANES, d), F32),
        ],
        compiler_params=pltpu.CompilerParams(
            dimension_semantics=("arbitrary", "arbitrary"), vmem_limit_bytes=VMEM_LIMIT,
            flags=MIXER_SCHEDULER_FLAGS),
        name="mixer_prompt",
    )(x, *wts)


def _mixer_sample_kernel(x_ref, h0_ref, cbuf_ref, pbuf_ref,
                         gmix_ref, win_ref, convw_ref, convb_ref, wbd_ref, bra_ref, brx_ref, lam_ref,
                         poolw_ref, pscale_ref, wpa_ref, wpb_ref, wout_ref,
                         x1_ref, h_ref, conv_ref, pool_ref, *, sl):
    sb, d = h0_ref.shape
    xs = [x_ref[:, l * d:(l + 1) * d] for l in range(sl)]
    x = jnp.concatenate(xs, axis=0)
    z = _in_proj(x, gmix_ref[...], win_ref[...])
    rows = lambda v, l: v[l * sb:(l + 1) * sb]
    xa = [rows(z[:, :d], l) for l in range(sl)]
    xb = [rows(z[:, d:2 * d], l) for l in range(sl)]
    ga, gb = z[:, 2 * d:3 * d], z[:, 3 * d:]

    cext = [cbuf_ref[:, k * d:(k + 1) * d] for k in range(CONV_WIDTH - 1)] + xa
    cw = convw_ref[...]
    xcs = []
    for l in range(sl):
        acc = convb_ref[...]
        for k in range(CONV_WIDTH):
            acc = acc + cext[l + k] * cw[k:k + 1]
        xcs.append(acc)
    xc = jnp.concatenate(xcs, axis=0)
    for k in range(CONV_WIDTH - 1):
        conv_ref[:, k * d:(k + 1) * d] = cext[sl + k]

    r, ig = _gates(xc, wbd_ref, bra_ref[...], brx_ref[...])
    a, mult = _lru_coeffs(r, lam_ref[...])
    bterm = mult * ig * xc
    h = h0_ref[...]
    hs = []
    for l in range(sl):
        bl = rows(bterm, l)
        if PAST_LEN + l == 0:
            bl = rows(ig * xc, l)
        h = rows(a, l) * h + bl
        hs.append(h)
    h_ref[...] = h

    gd = d // len(POOL_WINDOWS)
    pext = [pbuf_ref[:, k * d:(k + 1) * d] for k in range(POOL_BUF)] + xb
    for k in range(POOL_BUF):
        pool_ref[:, k * d:(k + 1) * d] = pext[sl + k]
    mixed_rows = []
    for l in range(sl):
        parts = []
        for g, win in enumerate(POOL_WINDOWS):
            sl_g = slice(g * gd, (g + 1) * gd)
            tot = pext[POOL_BUF + l][:, sl_g]
            for jj in range(1, win):
                tot = tot + pext[POOL_BUF + l - jj][:, sl_g]
            cnt = float(min(win, PAST_LEN + l + 1))
            parts.append(tot / cnt - xb[l][:, sl_g])
        mixed_rows.append(jnp.concatenate(parts, axis=-1))
    yb = _pool_proj(jnp.concatenate(mixed_rows, axis=0), poolw_ref, pscale_ref[...])

    x1_ref[...] = _merge_out(x, jnp.concatenate(hs, axis=0), yb, ga, gb,
                             wpa_ref[...], wpb_ref[...], wout_ref[...])


def _mixer_sample(x2d, h0, cbuf2d, pbuf2d, wts, sl):
    sb, d = h0.shape
    kern = functools.partial(_mixer_sample_kernel, sl=sl)
    ins = (x2d, h0, cbuf2d, pbuf2d) + tuple(wts)
    out_shape = (
        jax.ShapeDtypeStruct((sl * sb, d), F32),
        jax.ShapeDtypeStruct((sb, d), F32),
        jax.ShapeDtypeStruct(cbuf2d.shape, F32),
        jax.ShapeDtypeStruct(pbuf2d.shape, F32),
    )
    full = lambda s: pl.BlockSpec(s, lambda i: (0,) * len(s))
    return pl.pallas_call(
        kern,
        grid=(1,),
        in_specs=[full(v.shape) for v in ins],
        out_specs=tuple(full(s.shape) for s in out_shape),
        out_shape=out_shape,
        compiler_params=pltpu.CompilerParams(
            dimension_semantics=("arbitrary",), vmem_limit_bytes=VMEM_LIMIT),
        name="mixer_sample",
    )(*ins)


def _two_source_specs(tm, width, n_p_tiles):
    return [
        pl.BlockSpec((tm, width), lambda i, *_: (jnp.minimum(i, n_p_tiles - 1), 0)),
        pl.BlockSpec((tm, width), lambda i, *_: (jnp.maximum(i - n_p_tiles, 0), 0)),
    ]


def _router_kernel(xp_ref, xs_ref, g_ref, whi_ref, wlo_ref, br_ref, tri_ref,
                   idx_ref, rank_ref, wtok_ref, cnt_ref, base_ref, *, n_p_tiles):
    i = pl.program_id(0)
    tm = xp_ref.shape[0]

    @pl.when(i == 0)
    def _():
        base_ref[...] = jnp.zeros_like(base_ref)

    x = jnp.where(i < n_p_tiles, xp_ref[...], xs_ref[...])
    u = _rmsnorm(x, g_ref[...])
    u_hi = u.astype(BF16)
    u_lo = (u - u_hi.astype(F32)).astype(BF16)
    logits = _dot(u_hi, whi_ref[...]) + (_dot(u_hi, wlo_ref[...]) + _dot(u_lo, whi_ref[...]))
    lt = (logits + br_ref[...]).T[:N_EXPERTS]

    eio = lax.broadcasted_iota(I32, (N_EXPERTS, tm), 0)
    vals, idxs, sels = [], [], []
    cur = lt
    for _ in range(TOP_K):
        m = jnp.max(cur, axis=0, keepdims=True)
        ik = jnp.min(jnp.where(cur == m, eio, N_EXPERTS), axis=0, keepdims=True)
        sel = eio == ik
        vals.append(m)
        idxs.append(ik)
        sels.append(sel)
        cur = jnp.where(sel, -jnp.inf, cur)
    es = [jnp.exp(v - vals[0]) for v in vals]
    den = es[0]
    for e in es[1:]:
        den = den + e
    ws = [e / den for e in es]

    multi = sels[0].astype(F32)
    for s in sels[1:]:
        multi = multi + s.astype(F32)
    before = _dot(multi.astype(BF16), tri_ref[...]) + base_ref[:, 0:1]
    ranks = [jnp.sum(jnp.where(s, before, 0.0), axis=0, keepdims=True).astype(I32) for s in sels]

    idx_ref[...] = jnp.concatenate(idxs, axis=0)
    rank_ref[...] = jnp.concatenate(ranks, axis=0)
    wpad = jnp.concatenate(ws + [jnp.zeros((V7X_LANES - TOP_K, tm), F32)], axis=0)
    wtok_ref[...] = wpad.T
    new_base = base_ref[...] + jnp.sum(multi, axis=1, keepdims=True)
    base_ref[...] = new_base
    cnt_ref[...] = new_base.astype(I32)


def _router(x1p, x1s, g, whi, wlo, br, tm):
    t_p, d = x1p.shape
    t_s = x1s.shape[0]
    n_p_tiles, n_s_tiles = t_p // tm, t_s // tm
    t = t_p + t_s
    tri = (lax.broadcasted_iota(I32, (tm, tm), 0) < lax.broadcasted_iota(I32, (tm, tm), 1)).astype(BF16)
    kern = functools.partial(_router_kernel, n_p_tiles=n_p_tiles)
    out_shape = (
        jax.ShapeDtypeStruct((TOP_K, t), I32),
        jax.ShapeDtypeStruct((TOP_K, t), I32),
        jax.ShapeDtypeStruct((t, V7X_LANES), F32),
        jax.ShapeDtypeStruct((N_EXPERTS, V7X_LANES), I32),
    )
    consts = (g, whi, wlo, br, tri)
    return pl.pallas_call(
        kern,
        grid=(n_p_tiles + n_s_tiles,),
        in_specs=_two_source_specs(tm, d, n_p_tiles) + [_const_spec(c.shape) for c in consts],
        out_specs=(
            pl.BlockSpec((TOP_K, tm), lambda i: (0, i)),
            pl.BlockSpec((TOP_K, tm), lambda i: (0, i)),
            pl.BlockSpec((tm, V7X_LANES), lambda i: (i, 0)),
            pl.BlockSpec((N_EXPERTS, V7X_LANES), lambda i: (0, 0)),
        ),
        out_shape=out_shape,
        scratch_shapes=[pltpu.VMEM((N_EXPERTS, V7X_LANES), F32)],
        compiler_params=pltpu.CompilerParams(
            dimension_semantics=("arbitrary",), vmem_limit_bytes=VMEM_LIMIT),
        name="router",
    )(x1p, x1s, *consts)


def _row_copy(src, src_row, dst, dst_row, sem):
    return pltpu.make_async_copy(
        src.at[pl.ds(pl.multiple_of(src_row * ROW_CHUNKS, ROW_CHUNKS), ROW_CHUNKS), :],
        dst.at[pl.ds(pl.multiple_of(dst_row * ROW_CHUNKS, ROW_CHUNKS), ROW_CHUNKS), :],
        sem)


def _to_row_tiles(dst_ref, base, val):
    n = val.shape[0]
    for c in range(ROW_CHUNKS):
        dst_ref[pl.ds(base + c, n, stride=ROW_CHUNKS), :] = val[:, c * V7X_LANES:(c + 1) * V7X_LANES]


def _to_row_tiles_range(dst_ref, base, val, tmp_ref, first_row, lo, hi):
    n = val.shape[0]
    _to_row_tiles(tmp_ref, 0, val)
    tok = first_row + lax.shift_right_logical(
        lax.broadcasted_iota(I32, (n * ROW_CHUNKS, V7X_LANES), 0), ROW_CHUNKS.bit_length() - 1)
    mask = jnp.logical_and(tok >= lo, tok < hi)
    pltpu.store(dst_ref.at[pl.ds(base, n * ROW_CHUNKS), :], tmp_ref[...], mask=mask)


def _from_row_tiles(src_ref, base, n):
    return jnp.concatenate(
        [src_ref[pl.ds(base + c, n, stride=ROW_CHUNKS), :] for c in range(ROW_CHUNKS)], axis=-1)


def _issue_rows(n_tokens, issue_token):
    def trip(g, carry):
        for u in range(ISSUE_UNROLL):
            issue_token(g * ISSUE_UNROLL + u)
        return carry

    lax.fori_loop(0, n_tokens // ISSUE_UNROLL, trip, 0)


def _dispatch_kernel(pos_ref, xp_ref, xs_ref, g_ref, out_hbm, slab_ref, sems, *, n_p_tiles, n_tiles):
    i = pl.program_id(0)
    td = xp_ref.shape[0]
    slot = i % 2
    x = jnp.where(i < n_p_tiles, xp_ref[...], xs_ref[...])
    _to_row_tiles(slab_ref, slot * (td * ROW_CHUNKS), _rmsnorm(x, g_ref[...]))

    def issue_token(t):
        for k in range(TOP_K):
            _row_copy(slab_ref, slot * td + t, out_hbm, pos_ref[k, t], sems.at[slot]).start(priority=k % 2)

    _issue_rows(td, issue_token)

    def drain(s):
        for _ in range(TOP_K):
            pltpu.make_async_copy(slab_ref.at[pl.ds(0, td * ROW_CHUNKS), :],
                                  out_hbm.at[pl.ds(0, td * ROW_CHUNKS), :], sems.at[s]).wait()

    @pl.when(i > 0)
    def _():
        drain(1 - slot)

    @pl.when(i == n_tiles - 1)
    def _():
        drain(slot)


def _dispatch(pos, x1p, x1s, g, td):
    t_p, d = x1p.shape
    t_s = x1s.shape[0]
    n_p_tiles, n_s_tiles = t_p // td, t_s // td
    t = t_p + t_s
    kern = functools.partial(_dispatch_kernel, n_p_tiles=n_p_tiles, n_tiles=n_p_tiles + n_s_tiles)
    return pl.pallas_call(
        kern,
        grid=(n_p_tiles + n_s_tiles,),
        in_specs=[pl.BlockSpec((TOP_K, td), lambda i: (0, i), memory_space=pltpu.SMEM)]
        + _two_source_specs(td, d, n_p_tiles) + [_const_spec(g.shape)],
        out_specs=pl.BlockSpec(memory_space=pl.ANY),
        out_shape=jax.ShapeDtypeStruct((TOP_K * t * ROW_CHUNKS, V7X_LANES), F32),
        scratch_shapes=[pltpu.VMEM((2 * td * ROW_CHUNKS, V7X_LANES), F32), pltpu.SemaphoreType.DMA((2,))],
        compiler_params=pltpu.CompilerParams(
            dimension_semantics=("arbitrary",), vmem_limit_bytes=VMEM_LIMIT),
        name="dispatch",
    )(pos, x1p, x1s, g)


def _experts_kernel(tile_ref, exp_ref, lo_ref, hi_ref, wchg_ref,
                    xs_ref, wgu_ref, bgu_ref, wd_ref, bd_ref, ys_ref, wgu_s, wd_s, tmp_ref, *, tmx, sub):
    i = pl.program_id(0)
    de = wd_ref.shape[1]

    @pl.when(wchg_ref[i] == 1)
    def _():
        wgu_s[...] = wgu_ref[0].astype(BF16)
        wd_s[...] = wd_ref[0].astype(BF16)

    def mlp(first_row, n):
        x = _from_row_tiles(xs_ref, first_row * ROW_CHUNKS, n).astype(BF16)
        y = None
        for c0 in range(0, de, HIDDEN_CHUNK):
            c1 = c0 + HIDDEN_CHUNK
            gate = _dot(x, wgu_s[:, c0:c1]) + bgu_ref[0, :, c0:c1]
            up = _dot(x, wgu_s[:, de + c0:de + c1]) + bgu_ref[0, :, de + c0:de + c1]
            gate = jnp.minimum(gate, SWIGLU_LIMIT)
            up = jnp.clip(up, -SWIGLU_LIMIT, SWIGLU_LIMIT)
            h = (up + 1.0) * (gate * _sigmoid(SWIGLU_ALPHA * gate))
            part = _dot(h.astype(BF16), wd_s[c0:c1, :])
            y = part if y is None else y + part
        return y + bd_ref[0]

    lo, hi = lo_ref[i], hi_ref[i]
    t0 = tile_ref[i] * tmx
    owns_tile = jnp.logical_and(lo <= t0, hi >= t0 + tmx)

    @pl.when(owns_tile)
    def _():
        for r in range(0, tmx, EXPERT_HALF_ROWS):
            _to_row_tiles(ys_ref, r * ROW_CHUNKS, mlp(r, EXPERT_HALF_ROWS))

    for sb in range(tmx // sub):
        r0 = t0 + sb * sub
        overlaps = jnp.logical_and(hi > r0, lo < r0 + sub)

        @pl.when(jnp.logical_and(overlaps, jnp.logical_not(owns_tile)))
        def _(sb=sb, r0=r0):
            y = mlp(sb * sub, sub)
            whole = jnp.logical_and(lo <= r0, hi >= r0 + sub)

            @pl.when(whole)
            def _():
                _to_row_tiles(ys_ref, sb * sub * ROW_CHUNKS, y)

            @pl.when(jnp.logical_not(whole))
            def _():
                _to_row_tiles_range(ys_ref, sb * sub * ROW_CHUNKS, y, tmp_ref, r0, lo, hi)


def _experts(plan, xs, wgu, bgu, wd, bd, tmx, sub):
    n_rows = xs.shape[0] // ROW_CHUNKS
    n_work = plan[0].shape[0]
    _, d, de2 = wgu.shape
    de = wd.shape[1]
    kern = functools.partial(_experts_kernel, tmx=tmx, sub=sub)
    grid_spec = pltpu.PrefetchScalarGridSpec(
        num_scalar_prefetch=len(plan),
        grid=(n_work,),
        in_specs=[
            pl.BlockSpec((tmx * ROW_CHUNKS, V7X_LANES), lambda i, tile, *_: (tile[i], 0)),
            pl.BlockSpec((1, d, de2), lambda i, tile, ex, *_: (ex[i], 0, 0)),
            pl.BlockSpec((1, 1, de2), lambda i, tile, ex, *_: (ex[i], 0, 0)),
            pl.BlockSpec((1, de, d), lambda i, tile, ex, *_: (ex[i], 0, 0)),
            pl.BlockSpec((1, 1, d), lambda i, tile, ex, *_: (ex[i], 0, 0)),
        ],
        out_specs=pl.BlockSpec((tmx * ROW_CHUNKS, V7X_LANES), lambda i, tile, *_: (tile[i], 0)),
        scratch_shapes=[pltpu.VMEM((d, de2), BF16), pltpu.VMEM((de, d), BF16),
                        pltpu.VMEM((sub * ROW_CHUNKS, V7X_LANES), F32)],
    )
    return pl.pallas_call(
        kern,
        grid_spec=grid_spec,
        out_shape=jax.ShapeDtypeStruct((n_rows * ROW_CHUNKS, V7X_LANES), F32),
        compiler_params=pltpu.CompilerParams(
            dimension_semantics=("arbitrary",), vmem_limit_bytes=VMEM_LIMIT),
        name="experts",
    )(*plan, xs, wgu, bgu, wd, bd)


def _combine_kernel(pos_ref, posn_ref, ys_hbm, xp_ref, xs_ref, pp_ref, ps_ref, wtok_ref,
                    gple_ref, wgate_ref, wple_ref, gpost_ref, gfin_ref,
                    yp_ref, ysm_ref, gath_ref, sems, *, n_p_tiles, n_tiles):
    i = pl.program_id(0)
    tc = xp_ref.shape[0]
    slot = i % 2
    slot_rows = TOP_K * tc

    def gather(p_ref, s):
        def issue_token(t):
            for k in range(TOP_K):
                _row_copy(ys_hbm, p_ref[k, t], gath_ref, s * slot_rows + k * tc + t,
                          sems.at[s]).start(priority=k % 2)

        _issue_rows(tc, issue_token)

    @pl.when(i == 0)
    def _():
        gather(pos_ref, slot)

    @pl.when(i + 1 < n_tiles)
    def _():
        gather(posn_ref, 1 - slot)

    is_p = i < n_p_tiles
    x1 = jnp.where(is_p, xp_ref[...], xs_ref[...])
    p = jnp.where(is_p, pp_ref[...], ps_ref[...])
    ple = _rmsnorm(_dot(p.astype(BF16), wple_ref[...]), gpost_ref[...])
    base = slot * (slot_rows * ROW_CHUNKS)
    pltpu.make_async_copy(ys_hbm.at[pl.ds(0, slot_rows * ROW_CHUNKS), :],
                          gath_ref.at[pl.ds(base, slot_rows * ROW_CHUNKS), :], sems.at[slot]).wait()

    wt = wtok_ref[...]
    moe = wt[:, 0:1] * _from_row_tiles(gath_ref, base, tc)
    for k in range(1, TOP_K):
        moe = moe + wt[:, k:k + 1] * _from_row_tiles(gath_ref, base + k * tc * ROW_CHUNKS, tc)
    x2 = x1 + moe
    gate = _sigmoid(_dot(_rmsnorm(x2, gple_ref[...]).astype(BF16), wgate_ref[...]))
    y = _rmsnorm(x2 + ple * gate, gfin_ref[...])

    @pl.when(is_p)
    def _():
        yp_ref[...] = y

    @pl.when(jnp.logical_not(is_p))
    def _():
        ysm_ref[...] = y


def _combine(pos, ys, x1p, x1s, pp, ps, wtok, consts, tc):
    t_p, d = x1p.shape
    t_s = x1s.shape[0]
    n_p_tiles, n_s_tiles = t_p // tc, t_s // tc
    n_tiles = n_p_tiles + n_s_tiles
    kern = functools.partial(_combine_kernel, n_p_tiles=n_p_tiles, n_tiles=n_tiles)
    return pl.pallas_call(
        kern,
        grid=(n_tiles,),
        in_specs=[pl.BlockSpec((TOP_K, tc), lambda i: (0, i), memory_space=pltpu.SMEM),
                  pl.BlockSpec((TOP_K, tc), lambda i: (0, jnp.minimum(i + 1, n_tiles - 1)),
                               memory_space=pltpu.SMEM),
                  pl.BlockSpec(memory_space=pl.ANY)]
        + _two_source_specs(tc, d, n_p_tiles) + _two_source_specs(tc, pp.shape[1], n_p_tiles)
        + [pl.BlockSpec((tc, V7X_LANES), lambda i: (i, 0))] + [_const_spec(c.shape) for c in consts],
        out_specs=tuple(_two_source_specs(tc, d, n_p_tiles)),
        out_shape=(jax.ShapeDtypeStruct((t_p, d), F32), jax.ShapeDtypeStruct((t_s, d), F32)),
        scratch_shapes=[pltpu.VMEM((2 * TOP_K * tc * ROW_CHUNKS, V7X_LANES), F32),
                        pltpu.SemaphoreType.DMA((2,))],
        compiler_params=pltpu.CompilerParams(
            dimension_semantics=("arbitrary",), vmem_limit_bytes=VMEM_LIMIT),
        name="combine",
    )(pos, pos, ys, x1p, x1s, pp, ps, wtok, *consts)


def _plan(idx, rank, counts, tmx, n_work):
    eids = jnp.arange(N_EXPERTS, dtype=I32)
    incl = eids[None, :] <= eids[:, None]
    ends = jnp.sum(jnp.where(incl, counts[None, :], 0), axis=1)
    offs = ends - counts
    pos = rank + jnp.sum(jnp.where(idx[..., None] == eids, offs, 0), axis=-1)

    first_tile = offs // tmx
    last_tile = (ends - 1) // tmx
    n_e = jnp.where(counts > 0, last_tile - first_tile + 1, 0)
    iend = jnp.sum(jnp.where(incl, n_e[None, :], 0), axis=1)
    istart = iend - n_e
    total = iend[N_EXPERTS - 1]
    i = jnp.arange(n_work, dtype=I32)
    ic = jnp.minimum(i, total - 1)
    e_i = jnp.sum((ic[:, None] >= iend[None, :]).astype(I32), axis=1)
    onehot = e_i[:, None] == eids[None, :]
    pick = lambda v: jnp.sum(jnp.where(onehot, v[None, :], 0), axis=1)
    tile_i = pick(first_tile) + (ic - pick(istart))
    valid = i < total
    lo = jnp.where(valid, jnp.maximum(pick(offs), tile_i * tmx), 0)
    hi = jnp.where(valid, jnp.minimum(pick(ends), (tile_i + 1) * tmx), 0)
    prev_e = jnp.concatenate([jnp.full((1,), -1, I32), e_i[:-1]])
    wchg = (e_i != prev_e).astype(I32)
    return pos.astype(I32), tuple(v.astype(I32) for v in (tile_i, e_i, lo, hi, wchg))


def _block_diag_gates(w_a, w_x):
    heads, hd, _ = w_a.shape
    per = GATE_GROUP // hd
    groups = heads // per
    eye = jnp.eye(per, dtype=w_a.dtype)

    def bd(w):
        w4 = w.reshape(groups, per, hd, hd)
        return jnp.einsum('ghij,hk->ghikj', w4, eye).reshape(groups, GATE_GROUP, GATE_GROUP)

    return jnp.concatenate([bd(w_a), bd(w_x)], axis=-1).astype(BF16)


def _pick_tile(pref, *sizes):
    t = pref
    while any(s % t for s in sizes):
        t //= 2
    return t


def kernel(x_prompt, x_sample, p_prompt, p_sample, state_lru_h, state_conv, state_pool, g_mix, w_in, conv_w, conv_b, w_rg_a, b_rg_a, w_rg_x, b_rg_x, lru_lambda, pool_w, pool_scale, w_proj_a, w_proj_b, w_out, g_moe, w_router, b_router, w_gate_up, b_gate_up, w_down, b_down, g_ple, w_ple_gate, w_ple, g_ple_post, g_final):
    depth = g_mix.shape[0]
    assert depth == 1, "single-layer trunk"
    nb, seq, d = x_prompt.shape
    sb, sl, _ = x_sample.shape
    t_p, t_s = nb * seq, sb * sl
    t = t_p + t_s
    row = lambda v: v.reshape(1, -1)

    mixer_wts = (
        row(g_mix[0]), w_in[0].astype(BF16), conv_w[0], row(conv_b[0]),
        _block_diag_gates(w_rg_a[0], w_rg_x[0]), row(b_rg_a[0]), row(b_rg_x[0]), row(lru_lambda[0]),
        pool_w[0].astype(BF16), row(pool_scale[0]),
        w_proj_a[0].astype(BF16), w_proj_b[0].astype(BF16), w_out[0].astype(BF16),
    )

    tl = _pick_tile(256, seq)
    x1p, h_p, conv_p, pool_p = _mixer_prompt(x_prompt, mixer_wts, tl)
    x1p = x1p.reshape(t_p, d)
    x1s, h_s, conv_s, pool_s = _mixer_sample(
        x_sample.reshape(sb, sl * d), state_lru_h[0],
        state_conv[0].reshape(sb, (CONV_WIDTH - 1) * d), state_pool[0].reshape(sb, POOL_BUF * d),
        mixer_wts, sl)

    wr = jnp.pad(w_router[0], ((0, 0), (0, V7X_LANES - N_EXPERTS)))
    wr_hi = wr.astype(BF16)
    wr_lo = (wr - wr_hi.astype(F32)).astype(BF16)
    br = jnp.pad(b_router[0], (0, V7X_LANES - N_EXPERTS)).reshape(1, -1)
    tm = _pick_tile(512, t_p, t_s)
    idx, rank, wtok, cnt = _router(x1p, x1s, row(g_moe[0]), wr_hi, wr_lo, br, tm)

    tmx = _pick_tile(512, TOP_K * t)
    n_work = (TOP_K * t) // tmx + N_EXPERTS - 1
    pos, plan = _plan(idx, rank, cnt[:, 0], tmx, n_work)

    xs = _dispatch(pos, x1p, x1s, row(g_moe[0]), tm)
    ys = _experts(plan, xs, w_gate_up[0], b_gate_up[0][:, None, :], w_down[0], b_down[0][:, None, :], tmx,
                  _pick_tile(EXPERT_SUB_ROWS, tmx))

    pp = p_prompt[0].reshape(t_p, -1)
    ps = jnp.swapaxes(p_sample[0], 0, 1).reshape(t_s, -1)
    tc = _pick_tile(512, t_p, t_s)
    consts = (row(g_ple[0]), w_ple_gate[0].astype(BF16), w_ple[0].astype(BF16), row(g_ple_post[0]), row(g_final))
    y_p, y_s = _combine(pos, ys, x1p, x1s, pp, ps, wtok, consts, tc)

    y_prompt = y_p.reshape(nb, seq, d)
    y_sample = jnp.swapaxes(y_s.reshape(sl, sb, d), 0, 1)
    return (y_prompt, y_sample,
            h_p.reshape(depth, nb, d), conv_p[None], pool_p[None],
            h_s[None], conv_s.reshape(depth, sb, CONV_WIDTH - 1, d), pool_s.reshape(depth, sb, POOL_BUF, d))
```

```python
import functools

import jax
import jax.numpy as jnp
from jax import lax
from jax.experimental import pallas as pl
from jax.experimental.pallas import tpu as pltpu

F32 = jnp.float32
BF16 = jnp.bfloat16
I32 = jnp.int32

EPS = 1e-6
LRU_C = 8.0
LRU_HEADS = 16
CONV_WIDTH = 4
POOL_WINDOWS = (2, 4, 8, 16)
POOL_BUF = max(POOL_WINDOWS) - 1
N_EXPERTS = 32
TOP_K = 4
SWIGLU_LIMIT = 7.0
SWIGLU_ALPHA = 1.702
PAST_LEN = 16384

V7X_LANES = 128
V7X_SUBLANES = 8
V7X_VMEM_BYTES = 64 * 1024 * 1024
VMEM_LIMIT = V7X_VMEM_BYTES - 8 * 1024 * 1024

EXPERT_SUB_ROWS = 128
EXPERT_HALF_ROWS = 512
HIDDEN_CHUNK = 512
ISSUE_UNROLL = 8
MIXER_SEQS_PER_STEP = 2
MIXER_STAGE_LAG = 5
MIXER_SCHEDULER_FLAGS = None
GATE_GROUP = 256
ROW_CHUNKS = 8


def _rmsnorm(x, g):
    ms = jnp.mean(x * x, axis=-1, keepdims=True)
    return x * lax.rsqrt(ms + EPS) * g


def _sigmoid(x):
    return 1.0 / (1.0 + jnp.exp(-x))


def _softplus(x):
    return jnp.maximum(x, 0.0) + jnp.log1p(jnp.exp(-jnp.abs(x)))


def _dot(a, b):
    return jnp.dot(a, b, preferred_element_type=F32)


def _in_proj(x, g, w_in):
    return _dot(_rmsnorm(x, g).astype(BF16), w_in)


def _gates(xc, wbd_ref, bra, brx):
    xcb = xc.astype(BF16)
    n_groups = xc.shape[1] // GATE_GROUP
    rs, gs = [], []
    for g in range(n_groups):
        o = _dot(xcb[:, g * GATE_GROUP:(g + 1) * GATE_GROUP], wbd_ref[g])
        rs.append(o[:, :GATE_GROUP])
        gs.append(o[:, GATE_GROUP:])
    r = _sigmoid(jnp.concatenate(rs, axis=-1) + bra)
    ig = _sigmoid(jnp.concatenate(gs, axis=-1) + brx)
    return r, ig


def _lru_coeffs(r, lam):
    log_a = (-LRU_C * _softplus(-lam)) * r
    a = jnp.exp(log_a)
    th = jnp.tanh(log_a)
    q = -2.0 * th
    mult = jnp.where(q > 0.0, q * lax.rsqrt(q * (1.0 - th)), 0.0)
    return a, mult


def _scan_rows(a, b, h0):
    tl, w = a.shape
    groups = tl // V7X_SUBLANES
    a3 = a.reshape(groups, V7X_SUBLANES, w)
    b3 = b.reshape(groups, V7X_SUBLANES, w)
    sub = lax.broadcasted_iota(I32, (groups, V7X_SUBLANES, w), 1)
    for s in (1, 2, 4):
        a_sh = pltpu.roll(a3, s, 1)
        b_sh = pltpu.roll(b3, s, 1)
        valid = sub >= s
        b3 = jnp.where(valid, a3 * b_sh + b3, b3)
        a3 = jnp.where(valid, a3 * a_sh, a3)
    hs = []
    h = h0
    for g in range(groups):
        hg = a3[g] * h + b3[g]
        hs.append(hg)
        h = hg[V7X_SUBLANES - 1:V7X_SUBLANES, :]
    return jnp.concatenate(hs, axis=0), h


def _merge_out(x, h, yb, ga, gb, wpa, wpb, wout):
    pa = _dot(h.astype(BF16), wpa)
    pb = _dot(yb.astype(BF16), wpb)
    merged = _sigmoid(ga) * pa + _sigmoid(gb) * pb
    return x + _dot(merged.astype(BF16), wout)


def _pool_proj(mixed, poolw_ref, pscale):
    gd = mixed.shape[1] // len(POOL_WINDOWS)
    mb = mixed.astype(BF16)
    outs = [_dot(mb[:, g * gd:(g + 1) * gd], poolw_ref[g]) for g in range(len(POOL_WINDOWS))]
    return jnp.concatenate(outs, axis=-1) * pscale


def _mixer_prompt_kernel(x_ref, gmix_ref, win_ref, convw_ref, convb_ref, wbd_ref, bra_ref, brx_ref, lam_ref,
                         poolw_ref, pscale_ref, wpa_ref, wpb_ref, wout_ref,
                         x1_ref, h_ref, conv_ref, pool_ref,
                         hc_ref, cc_ref, pc_ref, *, tl, n_tiles):
    j = pl.program_id(1)
    d = x_ref.shape[-1]

    @pl.when(j == 0)
    def _():
        hc_ref[...] = jnp.zeros_like(hc_ref)
        cc_ref[...] = jnp.zeros_like(cc_ref)
        pc_ref[...] = jnp.zeros_like(pc_ref)

    tiles = [_mixer_prompt_tile(j, x_ref.at[q], gmix_ref, win_ref, convw_ref, convb_ref, wbd_ref, bra_ref,
                                brx_ref, lam_ref, poolw_ref, pscale_ref, wpa_ref, wpb_ref, wout_ref,
                                x1_ref.at[q], h_ref.at[q], conv_ref.at[q], pool_ref.at[q],
                                hc_ref.at[q], cc_ref.at[q], pc_ref.at[q], tl=tl, n_tiles=n_tiles)
             for q in range(x_ref.shape[0])]
    _interleave(tiles, MIXER_STAGE_LAG)


def _interleave(stage_iters, lag):
    live = dict(enumerate(stage_iters))
    step = 0
    while live:
        for q in sorted(live):
            if step >= q * lag:
                try:
                    next(live[q])
                except StopIteration:
                    del live[q]
        step += 1


def _mixer_prompt_tile(j, x_ref, gmix_ref, win_ref, convw_ref, convb_ref, wbd_ref, bra_ref, brx_ref, lam_ref,
                       poolw_ref, pscale_ref, wpa_ref, wpb_ref, wout_ref,
                       x1_ref, h_ref, conv_ref, pool_ref,
                       hc_ref, cc_ref, pc_ref, *, tl, n_tiles):
    d = x_ref.shape[-1]
    x = x_ref[...]
    u = _rmsnorm(x, gmix_ref[...]).astype(BF16)
    yield
    xa = _dot(u, win_ref[:, 0:d])
    yield
    xb = _dot(u, win_ref[:, d:2 * d])
    yield
    ga = _dot(u, win_ref[:, 2 * d:3 * d])
    yield
    gb = _dot(u, win_ref[:, 3 * d:4 * d])
    yield
    row = lax.broadcasted_iota(I32, (tl, 1), 0)

    full = jnp.concatenate([cc_ref[...], xa], axis=0)
    cw = convw_ref[...]
    xc = convb_ref[...]
    for k in range(CONV_WIDTH):
        s = CONV_WIDTH - 1 - k
        term = xa if s == 0 else pltpu.roll(full, s, 0)[V7X_SUBLANES:]
        xc = xc + term * cw[k:k + 1]
    yield

    r, ig = _gates(xc, wbd_ref, bra_ref[...], brx_ref[...])
    yield
    a, mult = _lru_coeffs(r, lam_ref[...])
    mult = jnp.where(jnp.logical_and(j == 0, row == 0), 1.0, mult)
    bterm = mult * ig * xc
    yield
    h, h_last = _scan_rows(a, bterm, hc_ref[0:1, :])
    yield

    gd = d // len(POOL_WINDOWS)
    ext = jnp.concatenate([pc_ref[...], xb], axis=0)
    pad = pc_ref.shape[0]
    s2 = ext + pltpu.roll(ext, 1, 0)
    s4 = s2[:, gd:] + pltpu.roll(s2[:, gd:], 2, 0)
    s8 = s4[:, gd:] + pltpu.roll(s4[:, gd:], 4, 0)
    s16 = s8[:, gd:] + pltpu.roll(s8[:, gd:], 8, 0)
    tots = [s2[pad:, :gd], s4[pad:, :gd], s8[pad:, :gd], s16[pad:, :]]
    pos = j * tl + row
    mixed = []
    for g, win in enumerate(POOL_WINDOWS):
        inv = 1.0 / jnp.minimum(win, pos + 1).astype(F32)
        mixed.append(tots[g] * inv - xb[:, g * gd:(g + 1) * gd])
    yield
    yb = _pool_proj(jnp.concatenate(mixed, axis=-1), poolw_ref, pscale_ref[...])
    sga = _sigmoid(ga)
    sgb = _sigmoid(gb)
    yield
    pa = _dot(h.astype(BF16), wpa_ref[...])
    yield
    pb = _dot(yb.astype(BF16), wpb_ref[...])
    merged = (sga * pa + sgb * pb).astype(BF16)
    yield
    x1_ref[...] = x + _dot(merged, wout_ref[...])

    hc_ref[...] = jnp.broadcast_to(h_last, hc_ref.shape)
    cc_ref[...] = xa[tl - cc_ref.shape[0]:, :]
    pc_ref[...] = xb[tl - pc_ref.shape[0]:, :]

    @pl.when(j == n_tiles - 1)
    def _():
        h_ref[...] = h_last
        conv_ref[...] = xa[tl - (CONV_WIDTH - 1):, :]
        pool_ref[...] = xb[tl - POOL_BUF:, :]


def _const_spec(shape):
    nd = len(shape)
    return pl.BlockSpec(shape, lambda *_: (0,) * nd, pipeline_mode=pl.Buffered(1))


def _mixer_weight_specs(wts):
    return [_const_spec(w.shape) for w in wts]


def _mixer_prompt(x, wts, tl):
    nb, seq, d = x.shape
    n_tiles = seq // tl
    nq = _pick_tile(MIXER_SEQS_PER_STEP, nb)
    kern = functools.partial(_mixer_prompt_kernel, tl=tl, n_tiles=n_tiles)
    out_shape = (
        jax.ShapeDtypeStruct((nb, seq, d), F32),
        jax.ShapeDtypeStruct((nb, 1, d), F32),
        jax.ShapeDtypeStruct((nb, CONV_WIDTH - 1, d), F32),
        jax.ShapeDtypeStruct((nb, POOL_BUF, d), F32),
    )
    return pl.pallas_call(
        kern,
        grid=(nb // nq, n_tiles),
        in_specs=[pl.BlockSpec((nq, tl, d), lambda b, j: (b, j, 0))] + _mixer_weight_specs(wts),
        out_specs=(
            pl.BlockSpec((nq, tl, d), lambda b, j: (b, j, 0)),
            pl.BlockSpec((nq, 1, d), lambda b, j: (b, 0, 0)),
            pl.BlockSpec((nq, CONV_WIDTH - 1, d), lambda b, j: (b, 0, 0)),
            pl.BlockSpec((nq, POOL_BUF, d), lambda b, j: (b, 0, 0)),
        ),
        out_shape=out_shape,
        scratch_shapes=[
            pltpu.VMEM((nq, V7X_SUBLANES, d), F32),
            pltpu.VMEM((nq, V7X_SUBLANES, d), F32),
            pltpu.VMEM((nq, 2 * V7X_SUBLANES, d), F32),
        ],
        compiler_params=pltpu.CompilerParams(
            dimension_semantics=("arbitrary", "arbitrary"), vmem_limit_bytes=VMEM_LIMIT,
            flags=MIXER_SCHEDULER_FLAGS),
        name="mixer_prompt",
    )(x, *wts)


def _mixer_sample_kernel(x_ref, h0_ref, cbuf_ref, pbuf_ref,
                         gmix_ref, win_ref, convw_ref, convb_ref, wbd_ref, bra_ref, brx_ref, lam_ref,
                         poolw_ref, pscale_ref, wpa_ref, wpb_ref, wout_ref,
                         x1_ref, h_ref, conv_ref, pool_ref, *, sl):
    sb, d = h0_ref.shape
    xs = [x_ref[:, l, :] for l in range(sl)]
    x = jnp.concatenate(xs, axis=0)
    z = _in_proj(x, gmix_ref[...], win_ref[...])
    rows = lambda v, l: v[l * sb:(l + 1) * sb]
    xa = [rows(z[:, :d], l) for l in range(sl)]
    xb = [rows(z[:, d:2 * d], l) for l in range(sl)]
    ga, gb = z[:, 2 * d:3 * d], z[:, 3 * d:]

    cext = [cbuf_ref[:, k, :] for k in range(CONV_WIDTH - 1)] + xa
    cw = convw_ref[...]
    xcs = []
    for l in range(sl):
        acc = convb_ref[...]
        for k in range(CONV_WIDTH):
            acc = acc + cext[l + k] * cw[k:k + 1]
        xcs.append(acc)
    xc = jnp.concatenate(xcs, axis=0)
    for k in range(CONV_WIDTH - 1):
        conv_ref[:, k, :] = cext[sl + k]

    r, ig = _gates(xc, wbd_ref, bra_ref[...], brx_ref[...])
    a, mult = _lru_coeffs(r, lam_ref[...])
    bterm = mult * ig * xc
    h = h0_ref[...]
    hs = []
    for l in range(sl):
        bl = rows(bterm, l)
        if PAST_LEN + l == 0:
            bl = rows(ig * xc, l)
        h = rows(a, l) * h + bl
        hs.append(h)
    h_ref[...] = h

    gd = d // len(POOL_WINDOWS)
    pext = [pbuf_ref[:, k, :] for k in range(POOL_BUF)] + xb
    for k in range(POOL_BUF):
        pool_ref[:, k, :] = pext[sl + k]
    mixed_rows = []
    for l in range(sl):
        parts = []
        for g, win in enumerate(POOL_WINDOWS):
            sl_g = slice(g * gd, (g + 1) * gd)
            tot = pext[POOL_BUF + l][:, sl_g]
            for jj in range(1, win):
                tot = tot + pext[POOL_BUF + l - jj][:, sl_g]
            cnt = float(min(win, PAST_LEN + l + 1))
            parts.append(tot / cnt - xb[l][:, sl_g])
        mixed_rows.append(jnp.concatenate(parts, axis=-1))
    yb = _pool_proj(jnp.concatenate(mixed_rows, axis=0), poolw_ref, pscale_ref[...])

    x1_ref[...] = _merge_out(x, jnp.concatenate(hs, axis=0), yb, ga, gb,
                             wpa_ref[...], wpb_ref[...], wout_ref[...])


def _mixer_sample(x2d, h0, cbuf2d, pbuf2d, wts, sl):
    sb, d = h0.shape
    kern = functools.partial(_mixer_sample_kernel, sl=sl)
    ins = (x2d, h0, cbuf2d, pbuf2d) + tuple(wts)
    out_shape = (
        jax.ShapeDtypeStruct((sl * sb, d), F32),
        jax.ShapeDtypeStruct((sb, d), F32),
        jax.ShapeDtypeStruct(cbuf2d.shape, F32),
        jax.ShapeDtypeStruct(pbuf2d.shape, F32),
    )
    full = lambda s: pl.BlockSpec(s, lambda i: (0,) * len(s))
    return pl.pallas_call(
        kern,
        grid=(1,),
        in_specs=[full(v.shape) for v in ins],
        out_specs=tuple(full(s.shape) for s in out_shape),
        out_shape=out_shape,
        compiler_params=pltpu.CompilerParams(
            dimension_semantics=("arbitrary",), vmem_limit_bytes=VMEM_LIMIT),
        name="mixer_sample",
    )(*ins)


def _two_source_specs(tm, width, n_p_tiles):
    return [
        pl.BlockSpec((tm, width), lambda i, *_: (jnp.minimum(i, n_p_tiles - 1), 0)),
        pl.BlockSpec((tm, width), lambda i, *_: (jnp.maximum(i - n_p_tiles, 0), 0)),
    ]


def _router_kernel(xp_ref, xs_ref, g_ref, whi_ref, wlo_ref, br_ref, tri_ref,
                   idx_ref, rank_ref, wtok_ref, cnt_ref, base_ref, *, n_p_tiles):
    i = pl.program_id(0)
    tm = xp_ref.shape[0]

    @pl.when(i == 0)
    def _():
        base_ref[...] = jnp.zeros_like(base_ref)

    x = jnp.where(i < n_p_tiles, xp_ref[...], xs_ref[...])
    u = _rmsnorm(x, g_ref[...])
    u_hi = u.astype(BF16)
    u_lo = (u - u_hi.astype(F32)).astype(BF16)
    logits = _dot(u_hi, whi_ref[...]) + (_dot(u_hi, wlo_ref[...]) + _dot(u_lo, whi_ref[...]))
    lt = (logits + br_ref[...]).T[:N_EXPERTS]

    eio = lax.broadcasted_iota(I32, (N_EXPERTS, tm), 0)
    vals, idxs, sels = [], [], []
    cur = lt
    for _ in range(TOP_K):
        m = jnp.max(cur, axis=0, keepdims=True)
        ik = jnp.min(jnp.where(cur == m, eio, N_EXPERTS), axis=0, keepdims=True)
        sel = eio == ik
        vals.append(m)
        idxs.append(ik)
        sels.append(sel)
        cur = jnp.where(sel, -jnp.inf, cur)
    es = [jnp.exp(v - vals[0]) for v in vals]
    den = es[0]
    for e in es[1:]:
        den = den + e
    ws = [e / den for e in es]

    multi = sels[0].astype(F32)
    for s in sels[1:]:
        multi = multi + s.astype(F32)
    before = _dot(multi.astype(BF16), tri_ref[...]) + base_ref[:, 0:1]
    ranks = [jnp.sum(jnp.where(s, before, 0.0), axis=0, keepdims=True).astype(I32) for s in sels]

    idx_ref[...] = jnp.concatenate(idxs, axis=0)
    rank_ref[...] = jnp.concatenate(ranks, axis=0)
    wpad = jnp.concatenate(ws + [jnp.zeros((V7X_LANES - TOP_K, tm), F32)], axis=0)
    wtok_ref[...] = wpad.T
    new_base = base_ref[...] + jnp.sum(multi, axis=1, keepdims=True)
    base_ref[...] = new_base
    cnt_ref[...] = new_base.astype(I32)


def _router(x1p, x1s, g, whi, wlo, br, tm):
    t_p, d = x1p.shape
    t_s = x1s.shape[0]
    n_p_tiles, n_s_tiles = t_p // tm, t_s // tm
    t = t_p + t_s
    tri = (lax.broadcasted_iota(I32, (tm, tm), 0) < lax.broadcasted_iota(I32, (tm, tm), 1)).astype(BF16)
    kern = functools.partial(_router_kernel, n_p_tiles=n_p_tiles)
    out_shape = (
        jax.ShapeDtypeStruct((TOP_K, t), I32),
        jax.ShapeDtypeStruct((TOP_K, t), I32),
        jax.ShapeDtypeStruct((t, V7X_LANES), F32),
        jax.ShapeDtypeStruct((N_EXPERTS, V7X_LANES), I32),
    )
    consts = (g, whi, wlo, br, tri)
    return pl.pallas_call(
        kern,
        grid=(n_p_tiles + n_s_tiles,),
        in_specs=_two_source_specs(tm, d, n_p_tiles) + [_const_spec(c.shape) for c in consts],
        out_specs=(
            pl.BlockSpec((TOP_K, tm), lambda i: (0, i)),
            pl.BlockSpec((TOP_K, tm), lambda i: (0, i)),
            pl.BlockSpec((tm, V7X_LANES), lambda i: (i, 0)),
            pl.BlockSpec((N_EXPERTS, V7X_LANES), lambda i: (0, 0)),
        ),
        out_shape=out_shape,
        scratch_shapes=[pltpu.VMEM((N_EXPERTS, V7X_LANES), F32)],
        compiler_params=pltpu.CompilerParams(
            dimension_semantics=("arbitrary",), vmem_limit_bytes=VMEM_LIMIT),
        name="router",
    )(x1p, x1s, *consts)


def _row_copy(src, src_row, dst, dst_row, sem):
    return pltpu.make_async_copy(
        src.at[pl.ds(pl.multiple_of(src_row * ROW_CHUNKS, ROW_CHUNKS), ROW_CHUNKS), :],
        dst.at[pl.ds(pl.multiple_of(dst_row * ROW_CHUNKS, ROW_CHUNKS), ROW_CHUNKS), :],
        sem)


def _to_row_tiles(dst_ref, base, val):
    n = val.shape[0]
    for c in range(ROW_CHUNKS):
        dst_ref[pl.ds(base + c, n, stride=ROW_CHUNKS), :] = val[:, c * V7X_LANES:(c + 1) * V7X_LANES]


def _to_row_tiles_range(dst_ref, base, val, tmp_ref, first_row, lo, hi):
    n = val.shape[0]
    _to_row_tiles(tmp_ref, 0, val)
    tok = first_row + lax.shift_right_logical(
        lax.broadcasted_iota(I32, (n * ROW_CHUNKS, V7X_LANES), 0), ROW_CHUNKS.bit_length() - 1)
    mask = jnp.logical_and(tok >= lo, tok < hi)
    pltpu.store(dst_ref.at[pl.ds(base, n * ROW_CHUNKS), :], tmp_ref[...], mask=mask)


def _from_row_tiles(src_ref, base, n):
    return jnp.concatenate(
        [src_ref[pl.ds(base + c, n, stride=ROW_CHUNKS), :] for c in range(ROW_CHUNKS)], axis=-1)


def _issue_rows(n_tokens, issue_token):
    def trip(g, carry):
        for u in range(ISSUE_UNROLL):
            issue_token(g * ISSUE_UNROLL + u)
        return carry

    lax.fori_loop(0, n_tokens // ISSUE_UNROLL, trip, 0)


def _dispatch_kernel(pos_ref, xp_ref, xs_ref, g_ref, out_hbm, slab_ref, sems, *, n_p_tiles, n_tiles):
    i = pl.program_id(0)
    td = xp_ref.shape[0]
    slot = i % 2
    x = jnp.where(i < n_p_tiles, xp_ref[...], xs_ref[...])
    _to_row_tiles(slab_ref, slot * (td * ROW_CHUNKS), _rmsnorm(x, g_ref[...]))

    def issue_token(t):
        for k in range(TOP_K):
            _row_copy(slab_ref, slot * td + t, out_hbm, pos_ref[k, t], sems.at[slot]).start(priority=k % 2)

    _issue_rows(td, issue_token)

    def drain(s):
        for _ in range(TOP_K):
            pltpu.make_async_copy(slab_ref.at[pl.ds(0, td * ROW_CHUNKS), :],
                                  out_hbm.at[pl.ds(0, td * ROW_CHUNKS), :], sems.at[s]).wait()

    @pl.when(i > 0)
    def _():
        drain(1 - slot)

    @pl.when(i == n_tiles - 1)
    def _():
        drain(slot)


def _dispatch(pos, x1p, x1s, g, td):
    t_p, d = x1p.shape
    t_s = x1s.shape[0]
    n_p_tiles, n_s_tiles = t_p // td, t_s // td
    t = t_p + t_s
    kern = functools.partial(_dispatch_kernel, n_p_tiles=n_p_tiles, n_tiles=n_p_tiles + n_s_tiles)
    return pl.pallas_call(
        kern,
        grid=(n_p_tiles + n_s_tiles,),
        in_specs=[pl.BlockSpec((TOP_K, td), lambda i: (0, i), memory_space=pltpu.SMEM)]
        + _two_source_specs(td, d, n_p_tiles) + [_const_spec(g.shape)],
        out_specs=pl.BlockSpec(memory_space=pl.ANY),
        out_shape=jax.ShapeDtypeStruct((TOP_K * t * ROW_CHUNKS, V7X_LANES), F32),
        scratch_shapes=[pltpu.VMEM((2 * td * ROW_CHUNKS, V7X_LANES), F32), pltpu.SemaphoreType.DMA((2,))],
        compiler_params=pltpu.CompilerParams(
            dimension_semantics=("arbitrary",), vmem_limit_bytes=VMEM_LIMIT),
        name="dispatch",
    )(pos, x1p, x1s, g)


def _experts_kernel(tile_ref, exp_ref, lo_ref, hi_ref, wchg_ref, first_ref, next_ref,
                    xs_ref, wgu_hbm, bgu_ref, wd_hbm, bd_ref, ys_ref,
                    wgu_s, wd_s, wgu_stage, wd_stage, tmp_ref, sems, *, tmx, sub):
    i = pl.program_id(0)
    de = wd_s.shape[0]

    def weight_copies(e):
        return (pltpu.make_async_copy(wgu_hbm.at[e], wgu_stage, sems.at[0]),
                pltpu.make_async_copy(wd_hbm.at[e], wd_stage, sems.at[1]))

    @pl.when(wchg_ref[i] == 1)
    def _():
        @pl.when(i == 0)
        def _():
            for cp in weight_copies(exp_ref[i]):
                cp.start()

        for cp in weight_copies(exp_ref[i]):
            cp.wait()
        wgu_s[...] = wgu_stage[...].astype(BF16)
        wd_s[...] = wd_stage[...].astype(BF16)

        @pl.when(next_ref[i] >= 0)
        def _():
            for cp in weight_copies(next_ref[i]):
                cp.start()

    def mlp(first_row, n):
        x = _from_row_tiles(xs_ref, first_row * ROW_CHUNKS, n).astype(BF16)
        y = None
        for c0 in range(0, de, HIDDEN_CHUNK):
            c1 = c0 + HIDDEN_CHUNK
            gate = _dot(x, wgu_s[:, c0:c1]) + bgu_ref[0, :, c0:c1]
            up = _dot(x, wgu_s[:, de + c0:de + c1]) + bgu_ref[0, :, de + c0:de + c1]
            gate = jnp.minimum(gate, SWIGLU_LIMIT)
            up = jnp.clip(up, -SWIGLU_LIMIT, SWIGLU_LIMIT)
            h = (up + 1.0) * (gate * _sigmoid(SWIGLU_ALPHA * gate))
            part = _dot(h.astype(BF16), wd_s[c0:c1, :])
            y = part if y is None else y + part
        return y + bd_ref[0]

    lo, hi = lo_ref[i], hi_ref[i]
    t0 = tile_ref[i] * tmx
    owns_tile = jnp.logical_and(lo <= t0, hi >= t0 + tmx)

    @pl.when(owns_tile)
    def _():
        for r in range(0, tmx, EXPERT_HALF_ROWS):
            _to_row_tiles(ys_ref, r * ROW_CHUNKS, mlp(r, EXPERT_HALF_ROWS))

    @pl.when(jnp.logical_and(first_ref[i] == 1, jnp.logical_and(hi > lo, jnp.logical_not(owns_tile))))
    def _():
        ys_ref[...] = jnp.zeros_like(ys_ref)

    for sb in range(tmx // sub):
        r0 = t0 + sb * sub
        overlaps = jnp.logical_and(hi > r0, lo < r0 + sub)

        @pl.when(jnp.logical_and(overlaps, jnp.logical_not(owns_tile)))
        def _(sb=sb, r0=r0):
            y = mlp(sb * sub, sub)
            whole = jnp.logical_and(lo <= r0, hi >= r0 + sub)

            @pl.when(whole)
            def _():
                _to_row_tiles(ys_ref, sb * sub * ROW_CHUNKS, y)

            @pl.when(jnp.logical_not(whole))
            def _():
                _to_row_tiles_range(ys_ref, sb * sub * ROW_CHUNKS, y, tmp_ref, r0, lo, hi)


def _experts(plan, xs, wgu, bgu, wd, bd, tmx, sub):
    n_rows = xs.shape[0] // ROW_CHUNKS
    n_work = plan[0].shape[0]
    _, d, de2 = wgu.shape
    de = wd.shape[1]
    kern = functools.partial(_experts_kernel, tmx=tmx, sub=sub)
    grid_spec = pltpu.PrefetchScalarGridSpec(
        num_scalar_prefetch=len(plan),
        grid=(n_work,),
        in_specs=[
            pl.BlockSpec((tmx * ROW_CHUNKS, V7X_LANES), lambda i, tile, *_: (tile[i], 0)),
            pl.BlockSpec(memory_space=pl.ANY),
            pl.BlockSpec((1, 1, de2), lambda i, tile, ex, *_: (ex[i], 0, 0)),
            pl.BlockSpec(memory_space=pl.ANY),
            pl.BlockSpec((1, 1, d), lambda i, tile, ex, *_: (ex[i], 0, 0)),
        ],
        out_specs=pl.BlockSpec((tmx * ROW_CHUNKS, V7X_LANES), lambda i, tile, *_: (tile[i], 0)),
        scratch_shapes=[pltpu.VMEM((d, de2), BF16), pltpu.VMEM((de, d), BF16),
                        pltpu.VMEM((d, de2), F32), pltpu.VMEM((de, d), F32),
                        pltpu.VMEM((sub * ROW_CHUNKS, V7X_LANES), F32),
                        pltpu.SemaphoreType.DMA((2,))],
    )
    return pl.pallas_call(
        kern,
        grid_spec=grid_spec,
        out_shape=jax.ShapeDtypeStruct((n_rows * ROW_CHUNKS, V7X_LANES), F32),
        compiler_params=pltpu.CompilerParams(
            dimension_semantics=("arbitrary",), vmem_limit_bytes=VMEM_LIMIT),
        name="experts",
    )(*plan, xs, wgu, bgu, wd, bd)


def _combine_kernel(pos_ref, posn_ref, ys_hbm, xp_ref, xs_ref, pp_ref, ps_ref, wtok_ref,
                    gple_ref, wgate_ref, wple_ref, gpost_ref, gfin_ref,
                    yp_ref, ysm_ref, gath_ref, sems, *, n_p_tiles, n_tiles):
    i = pl.program_id(0)
    tc = xp_ref.shape[0]
    slot = i % 2
    slot_rows = TOP_K * tc

    def gather(p_ref, s):
        def issue_token(t):
            for k in range(TOP_K):
                _row_copy(ys_hbm, p_ref[k, t], gath_ref, s * slot_rows + k * tc + t,
                          sems.at[s]).start(priority=k % 2)

        _issue_rows(tc, issue_token)

    @pl.when(i == 0)
    def _():
        gather(pos_ref, slot)

    @pl.when(i + 1 < n_tiles)
    def _():
        gather(posn_ref, 1 - slot)

    is_p = i < n_p_tiles
    x1 = jnp.where(is_p, xp_ref[...], xs_ref[...])
    p = jnp.where(is_p, pp_ref[...], ps_ref[...])
    ple = _rmsnorm(_dot(p.astype(BF16), wple_ref[...]), gpost_ref[...])
    base = slot * (slot_rows * ROW_CHUNKS)
    pltpu.make_async_copy(ys_hbm.at[pl.ds(0, slot_rows * ROW_CHUNKS), :],
                          gath_ref.at[pl.ds(base, slot_rows * ROW_CHUNKS), :], sems.at[slot]).wait()

    wt = wtok_ref[...]
    moe = wt[:, 0:1] * _from_row_tiles(gath_ref, base, tc)
    for k in range(1, TOP_K):
        moe = moe + wt[:, k:k + 1] * _from_row_tiles(gath_ref, base + k * tc * ROW_CHUNKS, tc)
    x2 = x1 + moe
    gate = _sigmoid(_dot(_rmsnorm(x2, gple_ref[...]).astype(BF16), wgate_ref[...]))
    y = _rmsnorm(x2 + ple * gate, gfin_ref[...])

    @pl.when(is_p)
    def _():
        yp_ref[...] = y

    @pl.when(jnp.logical_not(is_p))
    def _():
        ysm_ref[...] = y


def _combine(pos, ys, x1p, x1s, pp, ps, wtok, consts, tc):
    t_p, d = x1p.shape
    t_s = x1s.shape[0]
    n_p_tiles, n_s_tiles = t_p // tc, t_s // tc
    n_tiles = n_p_tiles + n_s_tiles
    kern = functools.partial(_combine_kernel, n_p_tiles=n_p_tiles, n_tiles=n_tiles)
    return pl.pallas_call(
        kern,
        grid=(n_tiles,),
        in_specs=[pl.BlockSpec((TOP_K, tc), lambda i: (0, i), memory_space=pltpu.SMEM),
                  pl.BlockSpec((TOP_K, tc), lambda i: (0, jnp.minimum(i + 1, n_tiles - 1)),
                               memory_space=pltpu.SMEM),
                  pl.BlockSpec(memory_space=pl.ANY)]
        + _two_source_specs(tc, d, n_p_tiles) + _two_source_specs(tc, pp.shape[1], n_p_tiles)
        + [pl.BlockSpec((tc, V7X_LANES), lambda i: (i, 0))] + [_const_spec(c.shape) for c in consts],
        out_specs=tuple(_two_source_specs(tc, d, n_p_tiles)),
        out_shape=(jax.ShapeDtypeStruct((t_p, d), F32), jax.ShapeDtypeStruct((t_s, d), F32)),
        scratch_shapes=[pltpu.VMEM((2 * TOP_K * tc * ROW_CHUNKS, V7X_LANES), F32),
                        pltpu.SemaphoreType.DMA((2,))],
        compiler_params=pltpu.CompilerParams(
            dimension_semantics=("arbitrary",), vmem_limit_bytes=VMEM_LIMIT),
        name="combine",
    )(pos, pos, ys, x1p, x1s, pp, ps, wtok, *consts)


def _plan(idx, rank, counts, tmx, n_work):
    eids = jnp.arange(N_EXPERTS, dtype=I32)
    incl = eids[None, :] <= eids[:, None]
    ends = jnp.sum(jnp.where(incl, counts[None, :], 0), axis=1)
    offs = ends - counts
    pos = rank + jnp.sum(jnp.where(idx[..., None] == eids, offs, 0), axis=-1)

    first_tile = offs // tmx
    last_tile = (ends - 1) // tmx
    n_e = jnp.where(counts > 0, last_tile - first_tile + 1, 0)
    iend = jnp.sum(jnp.where(incl, n_e[None, :], 0), axis=1)
    istart = iend - n_e
    total = iend[N_EXPERTS - 1]
    i = jnp.arange(n_work, dtype=I32)
    ic = jnp.minimum(i, total - 1)
    e_i = jnp.sum((ic[:, None] >= iend[None, :]).astype(I32), axis=1)
    onehot = e_i[:, None] == eids[None, :]
    pick = lambda v: jnp.sum(jnp.where(onehot, v[None, :], 0), axis=1)
    tile_i = pick(first_tile) + (ic - pick(istart))
    valid = i < total
    lo = jnp.where(valid, jnp.maximum(pick(offs), tile_i * tmx), 0)
    hi = jnp.where(valid, jnp.minimum(pick(ends), (tile_i + 1) * tmx), 0)
    prev_e = jnp.concatenate([jnp.full((1,), -1, I32), e_i[:-1]])
    prev_tile = jnp.concatenate([jnp.full((1,), -1, I32), tile_i[:-1]])
    wchg = (e_i != prev_e).astype(I32)
    first = (tile_i != prev_tile).astype(I32)
    later = jnp.logical_and(eids[None, :] > e_i[:, None], counts[None, :] > 0)
    nxt = jnp.min(jnp.where(later, eids[None, :], N_EXPERTS), axis=1)
    nxt = jnp.where(nxt < N_EXPERTS, nxt, -1)
    return pos.astype(I32), tuple(v.astype(I32) for v in (tile_i, e_i, lo, hi, wchg, first, nxt))


def _block_diag_gates(w_a, w_x):
    heads, hd, _ = w_a.shape
    per = GATE_GROUP // hd
    groups = heads // per
    eye = jnp.eye(per, dtype=w_a.dtype)

    def bd(w):
        w4 = w.reshape(groups, per, hd, hd)
        return jnp.einsum('ghij,hk->ghikj', w4, eye).reshape(groups, GATE_GROUP, GATE_GROUP)

    return jnp.concatenate([bd(w_a), bd(w_x)], axis=-1).astype(BF16)


def _pick_tile(pref, *sizes):
    t = pref
    while any(s % t for s in sizes):
        t //= 2
    return t


def kernel(x_prompt, x_sample, p_prompt, p_sample, state_lru_h, state_conv, state_pool, g_mix, w_in, conv_w, conv_b, w_rg_a, b_rg_a, w_rg_x, b_rg_x, lru_lambda, pool_w, pool_scale, w_proj_a, w_proj_b, w_out, g_moe, w_router, b_router, w_gate_up, b_gate_up, w_down, b_down, g_ple, w_ple_gate, w_ple, g_ple_post, g_final):
    depth = g_mix.shape[0]
    assert depth == 1, "single-layer trunk"
    nb, seq, d = x_prompt.shape
    sb, sl, _ = x_sample.shape
    t_p, t_s = nb * seq, sb * sl
    t = t_p + t_s
    row = lambda v: v.reshape(1, -1)

    mixer_wts = (
        row(g_mix[0]), w_in[0].astype(BF16), conv_w[0], row(conv_b[0]),
        _block_diag_gates(w_rg_a[0], w_rg_x[0]), row(b_rg_a[0]), row(b_rg_x[0]), row(lru_lambda[0]),
        pool_w[0].astype(BF16), row(pool_scale[0]),
        w_proj_a[0].astype(BF16), w_proj_b[0].astype(BF16), w_out[0].astype(BF16),
    )

    tl = _pick_tile(256, seq)
    x1p, h_p, conv_p, pool_p = _mixer_prompt(x_prompt, mixer_wts, tl)
    x1p = x1p.reshape(t_p, d)
    x1s, h_s, conv_s, pool_s = _mixer_sample(
        x_sample, state_lru_h[0], state_conv[0], state_pool[0], mixer_wts, sl)

    wr = jnp.pad(w_router[0], ((0, 0), (0, V7X_LANES - N_EXPERTS)))
    wr_hi = wr.astype(BF16)
    wr_lo = (wr - wr_hi.astype(F32)).astype(BF16)
    br = jnp.pad(b_router[0], (0, V7X_LANES - N_EXPERTS)).reshape(1, -1)
    tm = _pick_tile(512, t_p, t_s)
    idx, rank, wtok, cnt = _router(x1p, x1s, row(g_moe[0]), wr_hi, wr_lo, br, tm)

    tmx = _pick_tile(512, TOP_K * t)
    n_work = (TOP_K * t) // tmx + N_EXPERTS - 1
    pos, plan = _plan(idx, rank, cnt[:, 0], tmx, n_work)

    xs = _dispatch(pos, x1p, x1s, row(g_moe[0]), tm)
    ys = _experts(plan, xs, w_gate_up[0], b_gate_up[0][:, None, :], w_down[0], b_down[0][:, None, :], tmx,
                  _pick_tile(EXPERT_SUB_ROWS, tmx))

    pp = p_prompt[0].reshape(t_p, -1)
    ps = jnp.swapaxes(p_sample[0], 0, 1).reshape(t_s, -1)
    tc = _pick_tile(512, t_p, t_s)
    consts = (row(g_ple[0]), w_ple_gate[0].astype(BF16), w_ple[0].astype(BF16), row(g_ple_post[0]), row(g_final))
    y_p, y_s = _combine(pos, ys, x1p, x1s, pp, ps, wtok, consts, tc)

    y_prompt = y_p.reshape(nb, seq, d)
    y_sample = jnp.swapaxes(y_s.reshape(sl, sb, d), 0, 1)
    return (y_prompt, y_sample,
            h_p.reshape(depth, nb, d), conv_p[None], pool_p[None],
            h_s[None], conv_s[None], pool_s[None])
```

```python
import functools

import jax
import jax.numpy as jnp
from jax import lax
from jax.experimental import pallas as pl
from jax.experimental.pallas import tpu as pltpu

F32 = jnp.float32
BF16 = jnp.bfloat16
I32 = jnp.int32

EPS = 1e-6
LRU_C = 8.0
LRU_HEADS = 16
CONV_WIDTH = 4
POOL_WINDOWS = (2, 4, 8, 16)
POOL_BUF = max(POOL_WINDOWS) - 1
N_EXPERTS = 32
TOP_K = 4
SWIGLU_LIMIT = 7.0
SWIGLU_ALPHA = 1.702
PAST_LEN = 16384

V7X_LANES = 128
V7X_SUBLANES = 8
V7X_VMEM_BYTES = 64 * 1024 * 1024
VMEM_LIMIT = V7X_VMEM_BYTES - 8 * 1024 * 1024

EXPERT_SUB_ROWS = 128
EXPERT_HALF_ROWS = 512
HIDDEN_CHUNK = 512
ISSUE_UNROLL = 8
MIXER_SEQS_PER_STEP = 2
MIXER_STAGE_LAG = 5
MIXER_SCHEDULER_FLAGS = None
GATE_GROUP = 256
ROW_CHUNKS = 8


def _rmsnorm(x, g):
    ms = jnp.mean(x * x, axis=-1, keepdims=True)
    return x * lax.rsqrt(ms + EPS) * g


def _sigmoid(x):
    return 1.0 / (1.0 + jnp.exp(-x))


def _softplus(x):
    return jnp.maximum(x, 0.0) + jnp.log1p(jnp.exp(-jnp.abs(x)))


def _dot(a, b):
    return jnp.dot(a, b, preferred_element_type=F32)


def _in_proj(x, g, w_in):
    return _dot(_rmsnorm(x, g).astype(BF16), w_in)


def _gates(xc, wbd_ref, bra, brx):
    xcb = xc.astype(BF16)
    n_groups = xc.shape[1] // GATE_GROUP
    rs, gs = [], []
    for g in range(n_groups):
        o = _dot(xcb[:, g * GATE_GROUP:(g + 1) * GATE_GROUP], wbd_ref[g])
        rs.append(o[:, :GATE_GROUP])
        gs.append(o[:, GATE_GROUP:])
    r = _sigmoid(jnp.concatenate(rs, axis=-1) + bra)
    ig = _sigmoid(jnp.concatenate(gs, axis=-1) + brx)
    return r, ig


def _lru_coeffs(r, lam):
    log_a = (-LRU_C * _softplus(-lam)) * r
    a = jnp.exp(log_a)
    th = jnp.tanh(log_a)
    q = -2.0 * th
    mult = jnp.where(q > 0.0, q * lax.rsqrt(q * (1.0 - th)), 0.0)
    return a, mult


def _scan_rows(a, b, h0):
    tl, w = a.shape
    groups = tl // V7X_SUBLANES
    a3 = a.reshape(groups, V7X_SUBLANES, w)
    b3 = b.reshape(groups, V7X_SUBLANES, w)
    sub = lax.broadcasted_iota(I32, (groups, V7X_SUBLANES, w), 1)
    for s in (1, 2, 4):
        a_sh = pltpu.roll(a3, s, 1)
        b_sh = pltpu.roll(b3, s, 1)
        valid = sub >= s
        b3 = jnp.where(valid, a3 * b_sh + b3, b3)
        a3 = jnp.where(valid, a3 * a_sh, a3)
    hs = []
    h = h0
    for g in range(groups):
        hg = a3[g] * h + b3[g]
        hs.append(hg)
        h = hg[V7X_SUBLANES - 1:V7X_SUBLANES, :]
    return jnp.concatenate(hs, axis=0), h


def _merge_out(x, h, yb, ga, gb, wpa, wpb, wout):
    pa = _dot(h.astype(BF16), wpa)
    pb = _dot(yb.astype(BF16), wpb)
    merged = _sigmoid(ga) * pa + _sigmoid(gb) * pb
    return x + _dot(merged.astype(BF16), wout)


def _pool_proj(mixed, poolw_ref, pscale):
    gd = mixed.shape[1] // len(POOL_WINDOWS)
    mb = mixed.astype(BF16)
    outs = [_dot(mb[:, g * gd:(g + 1) * gd], poolw_ref[g]) for g in range(len(POOL_WINDOWS))]
    return jnp.concatenate(outs, axis=-1) * pscale


def _mixer_prompt_kernel(x_ref, gmix_ref, win_ref, convw_ref, convb_ref, wbd_ref, bra_ref, brx_ref, lam_ref,
                         poolw_ref, pscale_ref, wpa_ref, wpb_ref, wout_ref,
                         x1_ref, h_ref, conv_ref, pool_ref,
                         hc_ref, cc_ref, pc_ref, *, tl, n_tiles):
    j = pl.program_id(1)
    d = x_ref.shape[-1]

    @pl.when(j == 0)
    def _():
        hc_ref[...] = jnp.zeros_like(hc_ref)
        cc_ref[...] = jnp.zeros_like(cc_ref)
        pc_ref[...] = jnp.zeros_like(pc_ref)

    tiles = [_mixer_prompt_tile(j, x_ref.at[q], gmix_ref, win_ref, convw_ref, convb_ref, wbd_ref, bra_ref,
                                brx_ref, lam_ref, poolw_ref, pscale_ref, wpa_ref, wpb_ref, wout_ref,
                                x1_ref.at[q], h_ref.at[q], conv_ref.at[q], pool_ref.at[q],
                                hc_ref.at[q], cc_ref.at[q], pc_ref.at[q], tl=tl, n_tiles=n_tiles)
             for q in range(x_ref.shape[0])]
    _interleave(tiles, MIXER_STAGE_LAG)


def _interleave(stage_iters, lag):
    live = dict(enumerate(stage_iters))
    step = 0
    while live:
        for q in sorted(live):
            if step >= q * lag:
                try:
                    next(live[q])
                except StopIteration:
                    del live[q]
        step += 1


def _mixer_prompt_tile(j, x_ref, gmix_ref, win_ref, convw_ref, convb_ref, wbd_ref, bra_ref, brx_ref, lam_ref,
                       poolw_ref, pscale_ref, wpa_ref, wpb_ref, wout_ref,
                       x1_ref, h_ref, conv_ref, pool_ref,
                       hc_ref, cc_ref, pc_ref, *, tl, n_tiles):
    d = x_ref.shape[-1]
    x = x_ref[...]
    u = _rmsnorm(x, gmix_ref[...]).astype(BF16)
    yield
    xa = _dot(u, win_ref[:, 0:d])
    yield
    xb = _dot(u, win_ref[:, d:2 * d])
    yield
    ga = _dot(u, win_ref[:, 2 * d:3 * d])
    yield
    gb = _dot(u, win_ref[:, 3 * d:4 * d])
    yield
    row = lax.broadcasted_iota(I32, (tl, 1), 0)

    full = jnp.concatenate([cc_ref[...], xa], axis=0)
    cw = convw_ref[...]
    xc = convb_ref[...]
    for k in range(CONV_WIDTH):
        s = CONV_WIDTH - 1 - k
        term = xa if s == 0 else pltpu.roll(full, s, 0)[V7X_SUBLANES:]
        xc = xc + term * cw[k:k + 1]
    yield

    r, ig = _gates(xc, wbd_ref, bra_ref[...], brx_ref[...])
    yield
    a, mult = _lru_coeffs(r, lam_ref[...])
    mult = jnp.where(jnp.logical_and(j == 0, row == 0), 1.0, mult)
    bterm = mult * ig * xc
    yield
    h, h_last = _scan_rows(a, bterm, hc_ref[0:1, :])
    yield

    gd = d // len(POOL_WINDOWS)
    ext = jnp.concatenate([pc_ref[...], xb], axis=0)
    pad = pc_ref.shape[0]
    s2 = ext + pltpu.roll(ext, 1, 0)
    s4 = s2[:, gd:] + pltpu.roll(s2[:, gd:], 2, 0)
    s8 = s4[:, gd:] + pltpu.roll(s4[:, gd:], 4, 0)
    s16 = s8[:, gd:] + pltpu.roll(s8[:, gd:], 8, 0)
    tots = [s2[pad:, :gd], s4[pad:, :gd], s8[pad:, :gd], s16[pad:, :]]
    pos = j * tl + row
    mixed = []
    for g, win in enumerate(POOL_WINDOWS):
        inv = 1.0 / jnp.minimum(win, pos + 1).astype(F32)
        mixed.append(tots[g] * inv - xb[:, g * gd:(g + 1) * gd])
    yield
    yb = _pool_proj(jnp.concatenate(mixed, axis=-1), poolw_ref, pscale_ref[...])
    sga = _sigmoid(ga)
    sgb = _sigmoid(gb)
    yield
    pa = _dot(h.astype(BF16), wpa_ref[...])
    yield
    pb = _dot(yb.astype(BF16), wpb_ref[...])
    merged = (sga * pa + sgb * pb).astype(BF16)
    yield
    x1_ref[...] = x + _dot(merged, wout_ref[...])

    hc_ref[...] = jnp.broadcast_to(h_last, hc_ref.shape)
    cc_ref[...] = xa[tl - cc_ref.shape[0]:, :]
    pc_ref[...] = xb[tl - pc_ref.shape[0]:, :]

    @pl.when(j == n_tiles - 1)
    def _():
        h_ref[...] = h_last
        conv_ref[...] = xa[tl - (CONV_WIDTH - 1):, :]
        pool_ref[...] = xb[tl - POOL_BUF:, :]


def _const_spec(shape):
    nd = len(shape)
    return pl.BlockSpec(shape, lambda *_: (0,) * nd, pipeline_mode=pl.Buffered(1))


def _mixer_weight_specs(wts):
    return [_const_spec(w.shape) for w in wts]


def _mixer_prompt(x, wts, tl):
    nb, seq, d = x.shape
    n_tiles = seq // tl
    nq = _pick_tile(MIXER_SEQS_PER_STEP, nb)
    kern = functools.partial(_mixer_prompt_kernel, tl=tl, n_tiles=n_tiles)
    out_shape = (
        jax.ShapeDtypeStruct((nb, seq, d), F32),
        jax.ShapeDtypeStruct((nb, 1, d), F32),
        jax.ShapeDtypeStruct((nb, CONV_WIDTH - 1, d), F32),
        jax.ShapeDtypeStruct((nb, POOL_BUF, d), F32),
    )
    return pl.pallas_call(
        kern,
        grid=(nb // nq, n_tiles),
        in_specs=[pl.BlockSpec((nq, tl, d), lambda b, j: (b, j, 0))] + _mixer_weight_specs(wts),
        out_specs=(
            pl.BlockSpec((nq, tl, d), lambda b, j: (b, j, 0)),
            pl.BlockSpec((nq, 1, d), lambda b, j: (b, 0, 0)),
            pl.BlockSpec((nq, CONV_WIDTH - 1, d), lambda b, j: (b, 0, 0)),
            pl.BlockSpec((nq, POOL_BUF, d), lambda b, j: (b, 0, 0)),
        ),
        out_shape=out_shape,
        scratch_shapes=[
            pltpu.VMEM((nq, V7X_SUBLANES, d), F32),
            pltpu.VMEM((nq, V7X_SUBLANES, d), F32),
            pltpu.VMEM((nq, 2 * V7X_SUBLANES, d), F32),
        ],
        compiler_params=pltpu.CompilerParams(
            dimension_semantics=("arbitrary", "arbitrary"), vmem_limit_bytes=VMEM_LIMIT,
            flags=MIXER_SCHEDULER_FLAGS),
        name="mixer_prompt",
    )(x, *wts)


def _mixer_sample_kernel(x_ref, h0_ref, cbuf_ref, pbuf_ref,
                         gmix_ref, win_ref, convw_ref, convb_ref, wbd_ref, bra_ref, brx_ref, lam_ref,
                         poolw_ref, pscale_ref, wpa_ref, wpb_ref, wout_ref,
                         x1_ref, h_ref, conv_ref, pool_ref, *, sl):
    sb, d = h0_ref.shape
    xs = [x_ref[:, l, :] for l in range(sl)]
    x = jnp.concatenate(xs, axis=0)
    z = _in_proj(x, gmix_ref[...], win_ref[...])
    rows = lambda v, l: v[l * sb:(l + 1) * sb]
    xa = [rows(z[:, :d], l) for l in range(sl)]
    xb = [rows(z[:, d:2 * d], l) for l in range(sl)]
    ga, gb = z[:, 2 * d:3 * d], z[:, 3 * d:]

    cext = [cbuf_ref[:, k, :] for k in range(CONV_WIDTH - 1)] + xa
    cw = convw_ref[...]
    xcs = []
    for l in range(sl):
        acc = convb_ref[...]
        for k in range(CONV_WIDTH):
            acc = acc + cext[l + k] * cw[k:k + 1]
        xcs.append(acc)
    xc = jnp.concatenate(xcs, axis=0)
    for k in range(CONV_WIDTH - 1):
        conv_ref[:, k, :] = cext[sl + k]

    r, ig = _gates(xc, wbd_ref, bra_ref[...], brx_ref[...])
    a, mult = _lru_coeffs(r, lam_ref[...])
    bterm = mult * ig * xc
    h = h0_ref[...]
    hs = []
    for l in range(sl):
        bl = rows(bterm, l)
        if PAST_LEN + l == 0:
            bl = rows(ig * xc, l)
        h = rows(a, l) * h + bl
        hs.append(h)
    h_ref[...] = h

    gd = d // len(POOL_WINDOWS)
    pext = [pbuf_ref[:, k, :] for k in range(POOL_BUF)] + xb
    for k in range(POOL_BUF):
        pool_ref[:, k, :] = pext[sl + k]
    mixed_rows = []
    for l in range(sl):
        parts = []
        for g, win in enumerate(POOL_WINDOWS):
            sl_g = slice(g * gd, (g + 1) * gd)
            tot = pext[POOL_BUF + l][:, sl_g]
            for jj in range(1, win):
                tot = tot + pext[POOL_BUF + l - jj][:, sl_g]
            cnt = float(min(win, PAST_LEN + l + 1))
            parts.append(tot / cnt - xb[l][:, sl_g])
        mixed_rows.append(jnp.concatenate(parts, axis=-1))
    yb = _pool_proj(jnp.concatenate(mixed_rows, axis=0), poolw_ref, pscale_ref[...])

    x1_ref[...] = _merge_out(x, jnp.concatenate(hs, axis=0), yb, ga, gb,
                             wpa_ref[...], wpb_ref[...], wout_ref[...])


def _mixer_sample(x2d, h0, cbuf2d, pbuf2d, wts, sl):
    sb, d = h0.shape
    kern = functools.partial(_mixer_sample_kernel, sl=sl)
    ins = (x2d, h0, cbuf2d, pbuf2d) + tuple(wts)
    out_shape = (
        jax.ShapeDtypeStruct((sl * sb, d), F32),
        jax.ShapeDtypeStruct((sb, d), F32),
        jax.ShapeDtypeStruct(cbuf2d.shape, F32),
        jax.ShapeDtypeStruct(pbuf2d.shape, F32),
    )
    full = lambda s: pl.BlockSpec(s, lambda i: (0,) * len(s))
    return pl.pallas_call(
        kern,
        grid=(1,),
        in_specs=[full(v.shape) for v in ins],
        out_specs=tuple(full(s.shape) for s in out_shape),
        out_shape=out_shape,
        compiler_params=pltpu.CompilerParams(
            dimension_semantics=("arbitrary",), vmem_limit_bytes=VMEM_LIMIT),
        name="mixer_sample",
    )(*ins)


def _two_source_specs(tm, width, n_p_tiles):
    return [
        pl.BlockSpec((tm, width), lambda i, *_: (jnp.minimum(i, n_p_tiles - 1), 0)),
        pl.BlockSpec((tm, width), lambda i, *_: (jnp.maximum(i - n_p_tiles, 0), 0)),
    ]


def _router_kernel(xp_ref, xs_ref, g_ref, whi_ref, wlo_ref, br_ref, tri_ref,
                   idx_ref, rank_ref, wtok_ref, cnt_ref, base_ref, *, n_p_tiles):
    i = pl.program_id(0)
    tm = xp_ref.shape[0]

    @pl.when(i == 0)
    def _():
        base_ref[...] = jnp.zeros_like(base_ref)

    x = jnp.where(i < n_p_tiles, xp_ref[...], xs_ref[...])
    u = _rmsnorm(x, g_ref[...])
    u_hi = u.astype(BF16)
    u_lo = (u - u_hi.astype(F32)).astype(BF16)
    logits = _dot(u_hi, whi_ref[...]) + (_dot(u_hi, wlo_ref[...]) + _dot(u_lo, whi_ref[...]))
    lt = (logits + br_ref[...]).T[:N_EXPERTS]

    eio = lax.broadcasted_iota(I32, (N_EXPERTS, tm), 0)
    vals, idxs, sels = [], [], []
    cur = lt
    for _ in range(TOP_K):
        m = jnp.max(cur, axis=0, keepdims=True)
        ik = jnp.min(jnp.where(cur == m, eio, N_EXPERTS), axis=0, keepdims=True)
        sel = eio == ik
        vals.append(m)
        idxs.append(ik)
        sels.append(sel)
        cur = jnp.where(sel, -jnp.inf, cur)
    es = [jnp.exp(v - vals[0]) for v in vals]
    den = es[0]
    for e in es[1:]:
        den = den + e
    ws = [e / den for e in es]

    multi = sels[0].astype(F32)
    for s in sels[1:]:
        multi = multi + s.astype(F32)
    before = _dot(multi.astype(BF16), tri_ref[...]) + base_ref[:, 0:1]
    ranks = [jnp.sum(jnp.where(s, before, 0.0), axis=0, keepdims=True).astype(I32) for s in sels]

    idx_ref[...] = jnp.concatenate(idxs, axis=0)
    rank_ref[...] = jnp.concatenate(ranks, axis=0)
    wpad = jnp.concatenate(ws + [jnp.zeros((V7X_LANES - TOP_K, tm), F32)], axis=0)
    wtok_ref[...] = wpad.T
    new_base = base_ref[...] + jnp.sum(multi, axis=1, keepdims=True)
    base_ref[...] = new_base
    cnt_ref[...] = new_base.astype(I32)


def _router(x1p, x1s, g, whi, wlo, br, tm):
    t_p, d = x1p.shape
    t_s = x1s.shape[0]
    n_p_tiles, n_s_tiles = t_p // tm, t_s // tm
    t = t_p + t_s
    tri = (lax.broadcasted_iota(I32, (tm, tm), 0) < lax.broadcasted_iota(I32, (tm, tm), 1)).astype(BF16)
    kern = functools.partial(_router_kernel, n_p_tiles=n_p_tiles)
    out_shape = (
        jax.ShapeDtypeStruct((TOP_K, t), I32),
        jax.ShapeDtypeStruct((TOP_K, t), I32),
        jax.ShapeDtypeStruct((t, V7X_LANES), F32),
        jax.ShapeDtypeStruct((N_EXPERTS, V7X_LANES), I32),
    )
    consts = (g, whi, wlo, br, tri)
    return pl.pallas_call(
        kern,
        grid=(n_p_tiles + n_s_tiles,),
        in_specs=_two_source_specs(tm, d, n_p_tiles) + [_const_spec(c.shape) for c in consts],
        out_specs=(
            pl.BlockSpec((TOP_K, tm), lambda i: (0, i)),
            pl.BlockSpec((TOP_K, tm), lambda i: (0, i)),
            pl.BlockSpec((tm, V7X_LANES), lambda i: (i, 0)),
            pl.BlockSpec((N_EXPERTS, V7X_LANES), lambda i: (0, 0)),
        ),
        out_shape=out_shape,
        scratch_shapes=[pltpu.VMEM((N_EXPERTS, V7X_LANES), F32)],
        compiler_params=pltpu.CompilerParams(
            dimension_semantics=("arbitrary",), vmem_limit_bytes=VMEM_LIMIT),
        name="router",
    )(x1p, x1s, *consts)


def _row_copy(src, src_row, dst, dst_row, sem):
    return pltpu.make_async_copy(
        src.at[pl.ds(pl.multiple_of(src_row * ROW_CHUNKS, ROW_CHUNKS), ROW_CHUNKS), :],
        dst.at[pl.ds(pl.multiple_of(dst_row * ROW_CHUNKS, ROW_CHUNKS), ROW_CHUNKS), :],
        sem)


def _to_row_tiles(dst_ref, base, val):
    n = val.shape[0]
    for c in range(ROW_CHUNKS):
        dst_ref[pl.ds(base + c, n, stride=ROW_CHUNKS), :] = val[:, c * V7X_LANES:(c + 1) * V7X_LANES]


def _to_row_tiles_range(dst_ref, base, val, tmp_ref, first_row, lo, hi):
    n = val.shape[0]
    _to_row_tiles(tmp_ref, 0, val)
    tok = first_row + lax.shift_right_logical(
        lax.broadcasted_iota(I32, (n * ROW_CHUNKS, V7X_LANES), 0), ROW_CHUNKS.bit_length() - 1)
    mask = jnp.logical_and(tok >= lo, tok < hi)
    pltpu.store(dst_ref.at[pl.ds(base, n * ROW_CHUNKS), :], tmp_ref[...], mask=mask)


def _from_row_tiles(src_ref, base, n):
    return jnp.concatenate(
        [src_ref[pl.ds(base + c, n, stride=ROW_CHUNKS), :] for c in range(ROW_CHUNKS)], axis=-1)


def _issue_rows(n_tokens, issue_token):
    def trip(g, carry):
        for u in range(ISSUE_UNROLL):
            issue_token(g * ISSUE_UNROLL + u)
        return carry

    lax.fori_loop(0, n_tokens // ISSUE_UNROLL, trip, 0)


def _dispatch_kernel(pos_ref, xp_ref, xs_ref, g_ref, out_hbm, slab_ref, sems, *, n_p_tiles, n_tiles):
    i = pl.program_id(0)
    td = xp_ref.shape[0]
    slot = i % 2
    x = jnp.where(i < n_p_tiles, xp_ref[...], xs_ref[...])
    _to_row_tiles(slab_ref, slot * (td * ROW_CHUNKS), _rmsnorm(x, g_ref[...]))

    for t in range(td):
        for k in range(TOP_K):
            _row_copy(slab_ref, slot * td + t, out_hbm, pos_ref[k, t], sems.at[slot]).start(priority=k % 2)

    def drain(s):
        for _ in range(TOP_K):
            pltpu.make_async_copy(slab_ref.at[pl.ds(0, td * ROW_CHUNKS), :],
                                  out_hbm.at[pl.ds(0, td * ROW_CHUNKS), :], sems.at[s]).wait()

    @pl.when(i > 0)
    def _():
        drain(1 - slot)

    @pl.when(i == n_tiles - 1)
    def _():
        drain(slot)


def _dispatch(pos, x1p, x1s, g, td):
    t_p, d = x1p.shape
    t_s = x1s.shape[0]
    n_p_tiles, n_s_tiles = t_p // td, t_s // td
    t = t_p + t_s
    kern = functools.partial(_dispatch_kernel, n_p_tiles=n_p_tiles, n_tiles=n_p_tiles + n_s_tiles)
    return pl.pallas_call(
        kern,
        grid=(n_p_tiles + n_s_tiles,),
        in_specs=[pl.BlockSpec((TOP_K, td), lambda i: (0, i), memory_space=pltpu.SMEM)]
        + _two_source_specs(td, d, n_p_tiles) + [_const_spec(g.shape)],
        out_specs=pl.BlockSpec(memory_space=pl.ANY),
        out_shape=jax.ShapeDtypeStruct((TOP_K * t * ROW_CHUNKS, V7X_LANES), F32),
        scratch_shapes=[pltpu.VMEM((2 * td * ROW_CHUNKS, V7X_LANES), F32), pltpu.SemaphoreType.DMA((2,))],
        compiler_params=pltpu.CompilerParams(
            dimension_semantics=("arbitrary",), vmem_limit_bytes=VMEM_LIMIT),
        name="dispatch",
    )(pos, x1p, x1s, g)


def _experts_kernel(tile_ref, exp_ref, lo_ref, hi_ref, wchg_ref, first_ref, next_ref,
                    xs_ref, wgu_hbm, bgu_ref, wd_hbm, bd_ref, ys_ref,
                    wgu_s, wd_s, wgu_stage, wd_stage, tmp_ref, sems, *, tmx, sub):
    i = pl.program_id(0)
    de = wd_s.shape[0]

    def weight_copies(e):
        return (pltpu.make_async_copy(wgu_hbm.at[e], wgu_stage, sems.at[0]),
                pltpu.make_async_copy(wd_hbm.at[e], wd_stage, sems.at[1]))

    @pl.when(wchg_ref[i] == 1)
    def _():
        @pl.when(i == 0)
        def _():
            for cp in weight_copies(exp_ref[i]):
                cp.start()

        for cp in weight_copies(exp_ref[i]):
            cp.wait()
        wgu_s[...] = wgu_stage[...].astype(BF16)
        wd_s[...] = wd_stage[...].astype(BF16)

        @pl.when(next_ref[i] >= 0)
        def _():
            for cp in weight_copies(next_ref[i]):
                cp.start()

    def mlp(first_row, n):
        x = _from_row_tiles(xs_ref, first_row * ROW_CHUNKS, n).astype(BF16)
        y = None
        for c0 in range(0, de, HIDDEN_CHUNK):
            c1 = c0 + HIDDEN_CHUNK
            gate = _dot(x, wgu_s[:, c0:c1]) + bgu_ref[0, :, c0:c1]
            up = _dot(x, wgu_s[:, de + c0:de + c1]) + bgu_ref[0, :, de + c0:de + c1]
            gate = jnp.minimum(gate, SWIGLU_LIMIT)
            up = jnp.clip(up, -SWIGLU_LIMIT, SWIGLU_LIMIT)
            h = (up + 1.0) * (gate * _sigmoid(SWIGLU_ALPHA * gate))
            part = _dot(h.astype(BF16), wd_s[c0:c1, :])
            y = part if y is None else y + part
        return y + bd_ref[0]

    lo, hi = lo_ref[i], hi_ref[i]
    t0 = tile_ref[i] * tmx
    owns_tile = jnp.logical_and(lo <= t0, hi >= t0 + tmx)

    @pl.when(owns_tile)
    def _():
        for r in range(0, tmx, EXPERT_HALF_ROWS):
            _to_row_tiles(ys_ref, r * ROW_CHUNKS, mlp(r, EXPERT_HALF_ROWS))

    @pl.when(jnp.logical_and(first_ref[i] == 1, jnp.logical_and(hi > lo, jnp.logical_not(owns_tile))))
    def _():
        ys_ref[...] = jnp.zeros_like(ys_ref)

    for sb in range(tmx // sub):
        r0 = t0 + sb * sub
        overlaps = jnp.logical_and(hi > r0, lo < r0 + sub)

        @pl.when(jnp.logical_and(overlaps, jnp.logical_not(owns_tile)))
        def _(sb=sb, r0=r0):
            y = mlp(sb * sub, sub)
            whole = jnp.logical_and(lo <= r0, hi >= r0 + sub)

            @pl.when(whole)
            def _():
                _to_row_tiles(ys_ref, sb * sub * ROW_CHUNKS, y)

            @pl.when(jnp.logical_not(whole))
            def _():
                _to_row_tiles_range(ys_ref, sb * sub * ROW_CHUNKS, y, tmp_ref, r0, lo, hi)


def _experts(plan, xs, wgu, bgu, wd, bd, tmx, sub):
    n_rows = xs.shape[0] // ROW_CHUNKS
    n_work = plan[0].shape[0]
    _, d, de2 = wgu.shape
    de = wd.shape[1]
    kern = functools.partial(_experts_kernel, tmx=tmx, sub=sub)
    grid_spec = pltpu.PrefetchScalarGridSpec(
        num_scalar_prefetch=len(plan),
        grid=(n_work,),
        in_specs=[
            pl.BlockSpec((tmx * ROW_CHUNKS, V7X_LANES), lambda i, tile, *_: (tile[i], 0)),
            pl.BlockSpec(memory_space=pl.ANY),
            pl.BlockSpec((1, 1, de2), lambda i, tile, ex, *_: (ex[i], 0, 0)),
            pl.BlockSpec(memory_space=pl.ANY),
            pl.BlockSpec((1, 1, d), lambda i, tile, ex, *_: (ex[i], 0, 0)),
        ],
        out_specs=pl.BlockSpec((tmx * ROW_CHUNKS, V7X_LANES), lambda i, tile, *_: (tile[i], 0)),
        scratch_shapes=[pltpu.VMEM((d, de2), BF16), pltpu.VMEM((de, d), BF16),
                        pltpu.VMEM((d, de2), F32), pltpu.VMEM((de, d), F32),
                        pltpu.VMEM((sub * ROW_CHUNKS, V7X_LANES), F32),
                        pltpu.SemaphoreType.DMA((2,))],
    )
    return pl.pallas_call(
        kern,
        grid_spec=grid_spec,
        out_shape=jax.ShapeDtypeStruct((n_rows * ROW_CHUNKS, V7X_LANES), F32),
        compiler_params=pltpu.CompilerParams(
            dimension_semantics=("arbitrary",), vmem_limit_bytes=VMEM_LIMIT),
        name="experts",
    )(*plan, xs, wgu, bgu, wd, bd)


def _combine_kernel(pos_ref, posn_ref, ys_hbm, xp_ref, xs_ref, pp_ref, ps_ref, wtok_ref,
                    gple_ref, wgate_ref, wple_ref, gpost_ref, gfin_ref,
                    yp_ref, ysm_ref, gath_ref, sems, *, n_p_tiles, n_tiles):
    i = pl.program_id(0)
    tc = xp_ref.shape[0]
    slot = i % 2
    slot_rows = TOP_K * tc

    def gather(p_ref, s):
        def issue_token(t):
            for k in range(TOP_K):
                _row_copy(ys_hbm, p_ref[k, t], gath_ref, s * slot_rows + k * tc + t,
                          sems.at[s]).start(priority=k % 2)

        _issue_rows(tc, issue_token)

    @pl.when(i == 0)
    def _():
        gather(pos_ref, slot)

    for t in range(tc):
        for k in range(TOP_K):
            _row_copy(ys_hbm, posn_ref[k, t], gath_ref, (1 - slot) * slot_rows + k * tc + t,
                      sems.at[1 - slot]).start(priority=k % 2)

    is_p = i < n_p_tiles
    x1 = jnp.where(is_p, xp_ref[...], xs_ref[...])
    p = jnp.where(is_p, pp_ref[...], ps_ref[...])
    ple = _rmsnorm(_dot(p.astype(BF16), wple_ref[...]), gpost_ref[...])
    base = slot * (slot_rows * ROW_CHUNKS)
    pltpu.make_async_copy(ys_hbm.at[pl.ds(0, slot_rows * ROW_CHUNKS), :],
                          gath_ref.at[pl.ds(base, slot_rows * ROW_CHUNKS), :], sems.at[slot]).wait()

    wt = wtok_ref[...]
    moe = wt[:, 0:1] * _from_row_tiles(gath_ref, base, tc)
    for k in range(1, TOP_K):
        moe = moe + wt[:, k:k + 1] * _from_row_tiles(gath_ref, base + k * tc * ROW_CHUNKS, tc)
    x2 = x1 + moe
    gate = _sigmoid(_dot(_rmsnorm(x2, gple_ref[...]).astype(BF16), wgate_ref[...]))
    y = _rmsnorm(x2 + ple * gate, gfin_ref[...])

    @pl.when(is_p)
    def _():
        yp_ref[...] = y

    @pl.when(jnp.logical_not(is_p))
    def _():
        ysm_ref[...] = y

    @pl.when(i == n_tiles - 1)
    def _():
        spare = (1 - slot) * (slot_rows * ROW_CHUNKS)
        pltpu.make_async_copy(ys_hbm.at[pl.ds(0, slot_rows * ROW_CHUNKS), :],
                              gath_ref.at[pl.ds(spare, slot_rows * ROW_CHUNKS), :], sems.at[1 - slot]).wait()


def _combine(pos, ys, x1p, x1s, pp, ps, wtok, consts, tc):
    t_p, d = x1p.shape
    t_s = x1s.shape[0]
    n_p_tiles, n_s_tiles = t_p // tc, t_s // tc
    n_tiles = n_p_tiles + n_s_tiles
    kern = functools.partial(_combine_kernel, n_p_tiles=n_p_tiles, n_tiles=n_tiles)
    return pl.pallas_call(
        kern,
        grid=(n_tiles,),
        in_specs=[pl.BlockSpec((TOP_K, tc), lambda i: (0, i), memory_space=pltpu.SMEM),
                  pl.BlockSpec((TOP_K, tc), lambda i: (0, jnp.minimum(i + 1, n_tiles - 1)),
                               memory_space=pltpu.SMEM),
                  pl.BlockSpec(memory_space=pl.ANY)]
        + _two_source_specs(tc, d, n_p_tiles) + _two_source_specs(tc, pp.shape[1], n_p_tiles)
        + [pl.BlockSpec((tc, V7X_LANES), lambda i: (i, 0))] + [_const_spec(c.shape) for c in consts],
        out_specs=tuple(_two_source_specs(tc, d, n_p_tiles)),
        out_shape=(jax.ShapeDtypeStruct((t_p, d), F32), jax.ShapeDtypeStruct((t_s, d), F32)),
        scratch_shapes=[pltpu.VMEM((2 * TOP_K * tc * ROW_CHUNKS, V7X_LANES), F32),
                        pltpu.SemaphoreType.DMA((2,))],
        compiler_params=pltpu.CompilerParams(
            dimension_semantics=("arbitrary",), vmem_limit_bytes=VMEM_LIMIT),
        name="combine",
    )(pos, pos, ys, x1p, x1s, pp, ps, wtok, *consts)


def _plan(idx, rank, counts, tmx, n_work):
    eids = jnp.arange(N_EXPERTS, dtype=I32)
    incl = eids[None, :] <= eids[:, None]
    ends = jnp.sum(jnp.where(incl, counts[None, :], 0), axis=1)
    offs = ends - counts
    pos = rank + jnp.sum(jnp.where(idx[..., None] == eids, offs, 0), axis=-1)

    first_tile = offs // tmx
    last_tile = (ends - 1) // tmx
    n_e = jnp.where(counts > 0, last_tile - first_tile + 1, 0)
    iend = jnp.sum(jnp.where(incl, n_e[None, :], 0), axis=1)
    istart = iend - n_e
    total = iend[N_EXPERTS - 1]
    i = jnp.arange(n_work, dtype=I32)
    ic = jnp.minimum(i, total - 1)
    e_i = jnp.sum((ic[:, None] >= iend[None, :]).astype(I32), axis=1)
    onehot = e_i[:, None] == eids[None, :]
    pick = lambda v: jnp.sum(jnp.where(onehot, v[None, :], 0), axis=1)
    tile_i = pick(first_tile) + (ic - pick(istart))
    valid = i < total
    lo = jnp.where(valid, jnp.maximum(pick(offs), tile_i * tmx), 0)
    hi = jnp.where(valid, jnp.minimum(pick(ends), (tile_i + 1) * tmx), 0)
    prev_e = jnp.concatenate([jnp.full((1,), -1, I32), e_i[:-1]])
    prev_tile = jnp.concatenate([jnp.full((1,), -1, I32), tile_i[:-1]])
    wchg = (e_i != prev_e).astype(I32)
    first = (tile_i != prev_tile).astype(I32)
    later = jnp.logical_and(eids[None, :] > e_i[:, None], counts[None, :] > 0)
    nxt = jnp.min(jnp.where(later, eids[None, :], N_EXPERTS), axis=1)
    nxt = jnp.where(nxt < N_EXPERTS, nxt, -1)
    return pos.astype(I32), tuple(v.astype(I32) for v in (tile_i, e_i, lo, hi, wchg, first, nxt))


def _block_diag_gates(w_a, w_x):
    heads, hd, _ = w_a.shape
    per = GATE_GROUP // hd
    groups = heads // per
    eye = jnp.eye(per, dtype=w_a.dtype)

    def bd(w):
        w4 = w.reshape(groups, per, hd, hd)
        return jnp.einsum('ghij,hk->ghikj', w4, eye).reshape(groups, GATE_GROUP, GATE_GROUP)

    return jnp.concatenate([bd(w_a), bd(w_x)], axis=-1).astype(BF16)


def _pick_tile(pref, *sizes):
    t = pref
    while any(s % t for s in sizes):
        t //= 2
    return t


def kernel(x_prompt, x_sample, p_prompt, p_sample, state_lru_h, state_conv, state_pool, g_mix, w_in, conv_w, conv_b, w_rg_a, b_rg_a, w_rg_x, b_rg_x, lru_lambda, pool_w, pool_scale, w_proj_a, w_proj_b, w_out, g_moe, w_router, b_router, w_gate_up, b_gate_up, w_down, b_down, g_ple, w_ple_gate, w_ple, g_ple_post, g_final):
    depth = g_mix.shape[0]
    assert depth == 1, "single-layer trunk"
    nb, seq, d = x_prompt.shape
    sb, sl, _ = x_sample.shape
    t_p, t_s = nb * seq, sb * sl
    t = t_p + t_s
    row = lambda v: v.reshape(1, -1)

    mixer_wts = (
        row(g_mix[0]), w_in[0].astype(BF16), conv_w[0], row(conv_b[0]),
        _block_diag_gates(w_rg_a[0], w_rg_x[0]), row(b_rg_a[0]), row(b_rg_x[0]), row(lru_lambda[0]),
        pool_w[0].astype(BF16), row(pool_scale[0]),
        w_proj_a[0].astype(BF16), w_proj_b[0].astype(BF16), w_out[0].astype(BF16),
    )

    tl = _pick_tile(256, seq)
    x1p, h_p, conv_p, pool_p = _mixer_prompt(x_prompt, mixer_wts, tl)
    x1p = x1p.reshape(t_p, d)
    x1s, h_s, conv_s, pool_s = _mixer_sample(
        x_sample, state_lru_h[0], state_conv[0], state_pool[0], mixer_wts, sl)

    wr = jnp.pad(w_router[0], ((0, 0), (0, V7X_LANES - N_EXPERTS)))
    wr_hi = wr.astype(BF16)
    wr_lo = (wr - wr_hi.astype(F32)).astype(BF16)
    br = jnp.pad(b_router[0], (0, V7X_LANES - N_EXPERTS)).reshape(1, -1)
    tm = _pick_tile(512, t_p, t_s)
    idx, rank, wtok, cnt = _router(x1p, x1s, row(g_moe[0]), wr_hi, wr_lo, br, tm)

    tmx = _pick_tile(512, TOP_K * t)
    n_work = (TOP_K * t) // tmx + N_EXPERTS - 1
    pos, plan = _plan(idx, rank, cnt[:, 0], tmx, n_work)

    xs = _dispatch(pos, x1p, x1s, row(g_moe[0]), tm)
    ys = _experts(plan, xs, w_gate_up[0], b_gate_up[0][:, None, :], w_down[0], b_down[0][:, None, :], tmx,
                  _pick_tile(EXPERT_SUB_ROWS, tmx))

    pp = p_prompt[0].reshape(t_p, -1)
    ps = jnp.swapaxes(p_sample[0], 0, 1).reshape(t_s, -1)
    tc = _pick_tile(512, t_p, t_s)
    consts = (row(g_ple[0]), w_ple_gate[0].astype(BF16), w_ple[0].astype(BF16), row(g_ple_post[0]), row(g_final))
    y_p, y_s = _combine(pos, ys, x1p, x1s, pp, ps, wtok, consts, tc)

    y_prompt = y_p.reshape(nb, seq, d)
    y_sample = jnp.swapaxes(y_s.reshape(sl, sb, d), 0, 1)
    return (y_prompt, y_sample,
            h_p.reshape(depth, nb, d), conv_p[None], pool_p[None],
            h_s[None], conv_s[None], pool_s[None])
```

```python
import functools

import jax
import jax.numpy as jnp
from jax import lax
from jax.experimental import pallas as pl
from jax.experimental.pallas import tpu as pltpu

F32 = jnp.float32
BF16 = jnp.bfloat16
I32 = jnp.int32

EPS = 1e-6
LRU_C = 8.0
LRU_HEADS = 16
CONV_WIDTH = 4
POOL_WINDOWS = (2, 4, 8, 16)
POOL_BUF = max(POOL_WINDOWS) - 1
N_EXPERTS = 32
TOP_K = 4
SWIGLU_LIMIT = 7.0
SWIGLU_ALPHA = 1.702
PAST_LEN = 16384

V7X_LANES = 128
V7X_SUBLANES = 8
V7X_VMEM_BYTES = 64 * 1024 * 1024
VMEM_LIMIT = V7X_VMEM_BYTES - 8 * 1024 * 1024

EXPERT_SUB_ROWS = 128
EXPERT_HALF_ROWS = 512
HIDDEN_CHUNK = 512
ISSUE_UNROLL = 8
COMBINE_SUB_ROWS = 128
MIXER_SEQS_PER_STEP = 2
MIXER_STAGE_LAG = 5
MIXER_SCHEDULER_FLAGS = None
GATE_GROUP = 256
ROW_CHUNKS = 8


def _rmsnorm(x, g):
    ms = jnp.mean(x * x, axis=-1, keepdims=True)
    return x * lax.rsqrt(ms + EPS) * g


def _sigmoid(x):
    return 1.0 / (1.0 + jnp.exp(-x))


def _softplus(x):
    return jnp.maximum(x, 0.0) + jnp.log1p(jnp.exp(-jnp.abs(x)))


def _dot(a, b):
    return jnp.dot(a, b, preferred_element_type=F32)


def _in_proj(x, g, w_in):
    return _dot(_rmsnorm(x, g).astype(BF16), w_in)


def _gates(xc, wbd_ref, bra, brx):
    xcb = xc.astype(BF16)
    n_groups = xc.shape[1] // GATE_GROUP
    rs, gs = [], []
    for g in range(n_groups):
        o = _dot(xcb[:, g * GATE_GROUP:(g + 1) * GATE_GROUP], wbd_ref[g])
        rs.append(o[:, :GATE_GROUP])
        gs.append(o[:, GATE_GROUP:])
    r = _sigmoid(jnp.concatenate(rs, axis=-1) + bra)
    ig = _sigmoid(jnp.concatenate(gs, axis=-1) + brx)
    return r, ig


def _lru_coeffs(r, lam):
    log_a = (-LRU_C * _softplus(-lam)) * r
    a = jnp.exp(log_a)
    th = jnp.tanh(log_a)
    q = -2.0 * th
    mult = jnp.where(q > 0.0, q * lax.rsqrt(q * (1.0 - th)), 0.0)
    return a, mult


def _scan_rows(a, b, h0):
    tl, w = a.shape
    groups = tl // V7X_SUBLANES
    a3 = a.reshape(groups, V7X_SUBLANES, w)
    b3 = b.reshape(groups, V7X_SUBLANES, w)
    sub = lax.broadcasted_iota(I32, (groups, V7X_SUBLANES, w), 1)
    for s in (1, 2, 4):
        a_sh = pltpu.roll(a3, s, 1)
        b_sh = pltpu.roll(b3, s, 1)
        valid = sub >= s
        b3 = jnp.where(valid, a3 * b_sh + b3, b3)
        a3 = jnp.where(valid, a3 * a_sh, a3)
    hs = []
    h = h0
    for g in range(groups):
        hg = a3[g] * h + b3[g]
        hs.append(hg)
        h = hg[V7X_SUBLANES - 1:V7X_SUBLANES, :]
    return jnp.concatenate(hs, axis=0), h


def _merge_out(x, h, yb, ga, gb, wpa, wpb, wout):
    pa = _dot(h.astype(BF16), wpa)
    pb = _dot(yb.astype(BF16), wpb)
    merged = _sigmoid(ga) * pa + _sigmoid(gb) * pb
    return x + _dot(merged.astype(BF16), wout)


def _pool_proj(mixed, poolw_ref, pscale):
    gd = mixed.shape[1] // len(POOL_WINDOWS)
    mb = mixed.astype(BF16)
    outs = [_dot(mb[:, g * gd:(g + 1) * gd], poolw_ref[g]) for g in range(len(POOL_WINDOWS))]
    return jnp.concatenate(outs, axis=-1) * pscale


def _mixer_prompt_kernel(x_ref, gmix_ref, win_ref, convw_ref, convb_ref, wbd_ref, bra_ref, brx_ref, lam_ref,
                         poolw_ref, pscale_ref, wpa_ref, wpb_ref, wout_ref,
                         x1_ref, h_ref, conv_ref, pool_ref,
                         hc_ref, cc_ref, pc_ref, *, tl, n_tiles):
    j = pl.program_id(1)
    d = x_ref.shape[-1]

    @pl.when(j == 0)
    def _():
        hc_ref[...] = jnp.zeros_like(hc_ref)
        cc_ref[...] = jnp.zeros_like(cc_ref)
        pc_ref[...] = jnp.zeros_like(pc_ref)

    tiles = [_mixer_prompt_tile(j, x_ref.at[q], gmix_ref, win_ref, convw_ref, convb_ref, wbd_ref, bra_ref,
                                brx_ref, lam_ref, poolw_ref, pscale_ref, wpa_ref, wpb_ref, wout_ref,
                                x1_ref.at[q], h_ref.at[q], conv_ref.at[q], pool_ref.at[q],
                                hc_ref.at[q], cc_ref.at[q], pc_ref.at[q], tl=tl, n_tiles=n_tiles)
             for q in range(x_ref.shape[0])]
    _interleave(tiles, MIXER_STAGE_LAG)


def _interleave(stage_iters, lag):
    live = dict(enumerate(stage_iters))
    step = 0
    while live:
        for q in sorted(live):
            if step >= q * lag:
                try:
                    next(live[q])
                except StopIteration:
                    del live[q]
        step += 1


def _mixer_prompt_tile(j, x_ref, gmix_ref, win_ref, convw_ref, convb_ref, wbd_ref, bra_ref, brx_ref, lam_ref,
                       poolw_ref, pscale_ref, wpa_ref, wpb_ref, wout_ref,
                       x1_ref, h_ref, conv_ref, pool_ref,
                       hc_ref, cc_ref, pc_ref, *, tl, n_tiles):
    d = x_ref.shape[-1]
    x = x_ref[...]
    u = _rmsnorm(x, gmix_ref[...]).astype(BF16)
    yield
    xa = _dot(u, win_ref[:, 0:d])
    yield
    xb = _dot(u, win_ref[:, d:2 * d])
    yield
    ga = _dot(u, win_ref[:, 2 * d:3 * d])
    yield
    gb = _dot(u, win_ref[:, 3 * d:4 * d])
    yield
    row = lax.broadcasted_iota(I32, (tl, 1), 0)

    full = jnp.concatenate([cc_ref[...], xa], axis=0)
    cw = convw_ref[...]
    xc = convb_ref[...]
    for k in range(CONV_WIDTH):
        s = CONV_WIDTH - 1 - k
        term = xa if s == 0 else pltpu.roll(full, s, 0)[V7X_SUBLANES:]
        xc = xc + term * cw[k:k + 1]
    yield

    r, ig = _gates(xc, wbd_ref, bra_ref[...], brx_ref[...])
    yield
    a, mult = _lru_coeffs(r, lam_ref[...])
    mult = jnp.where(jnp.logical_and(j == 0, row == 0), 1.0, mult)
    bterm = mult * ig * xc
    yield
    h, h_last = _scan_rows(a, bterm, hc_ref[0:1, :])
    yield

    gd = d // len(POOL_WINDOWS)
    ext = jnp.concatenate([pc_ref[...], xb], axis=0)
    pad = pc_ref.shape[0]
    s2 = ext + pltpu.roll(ext, 1, 0)
    s4 = s2[:, gd:] + pltpu.roll(s2[:, gd:], 2, 0)
    s8 = s4[:, gd:] + pltpu.roll(s4[:, gd:], 4, 0)
    s16 = s8[:, gd:] + pltpu.roll(s8[:, gd:], 8, 0)
    tots = [s2[pad:, :gd], s4[pad:, :gd], s8[pad:, :gd], s16[pad:, :]]
    pos = j * tl + row
    mixed = []
    for g, win in enumerate(POOL_WINDOWS):
        inv = 1.0 / jnp.minimum(win, pos + 1).astype(F32)
        mixed.append(tots[g] * inv - xb[:, g * gd:(g + 1) * gd])
    yield
    yb = _pool_proj(jnp.concatenate(mixed, axis=-1), poolw_ref, pscale_ref[...])
    sga = _sigmoid(ga)
    sgb = _sigmoid(gb)
    yield
    pa = _dot(h.astype(BF16), wpa_ref[...])
    yield
    pb = _dot(yb.astype(BF16), wpb_ref[...])
    merged = (sga * pa + sgb * pb).astype(BF16)
    yield
    x1_ref[...] = x + _dot(merged, wout_ref[...])

    hc_ref[...] = jnp.broadcast_to(h_last, hc_ref.shape)
    cc_ref[...] = xa[tl - cc_ref.shape[0]:, :]
    pc_ref[...] = xb[tl - pc_ref.shape[0]:, :]

    @pl.when(j == n_tiles - 1)
    def _():
        h_ref[...] = h_last
        conv_ref[...] = xa[tl - (CONV_WIDTH - 1):, :]
        pool_ref[...] = xb[tl - POOL_BUF:, :]


def _const_spec(shape):
    nd = len(shape)
    return pl.BlockSpec(shape, lambda *_: (0,) * nd, pipeline_mode=pl.Buffered(1))


def _mixer_weight_specs(wts):
    return [_const_spec(w.shape) for w in wts]


def _mixer_prompt(x, wts, tl):
    nb, seq, d = x.shape
    n_tiles = seq // tl
    nq = _pick_tile(MIXER_SEQS_PER_STEP, nb)
    kern = functools.partial(_mixer_prompt_kernel, tl=tl, n_tiles=n_tiles)
    out_shape = (
        jax.ShapeDtypeStruct((nb, seq, d), F32),
        jax.ShapeDtypeStruct((nb, 1, d), F32),
        jax.ShapeDtypeStruct((nb, CONV_WIDTH - 1, d), F32),
        jax.ShapeDtypeStruct((nb, POOL_BUF, d), F32),
    )
    return pl.pallas_call(
        kern,
        grid=(nb // nq, n_tiles),
        in_specs=[pl.BlockSpec((nq, tl, d), lambda b, j: (b, j, 0))] + _mixer_weight_specs(wts),
        out_specs=(
            pl.BlockSpec((nq, tl, d), lambda b, j: (b, j, 0)),
            pl.BlockSpec((nq, 1, d), lambda b, j: (b, 0, 0)),
            pl.BlockSpec((nq, CONV_WIDTH - 1, d), lambda b, j: (b, 0, 0)),
            pl.BlockSpec((nq, POOL_BUF, d), lambda b, j: (b, 0, 0)),
        ),
        out_shape=out_shape,
        scratch_shapes=[
            pltpu.VMEM((nq, V7X_SUBLANES, d), F32),
            pltpu.VMEM((nq, V7X_SUBLANES, d), F32),
            pltpu.VMEM((nq, 2 * V7X_SUBLANES, d), F32),
        ],
        compiler_params=pltpu.CompilerParams(
            dimension_semantics=("arbitrary", "arbitrary"), vmem_limit_bytes=VMEM_LIMIT,
            flags=MIXER_SCHEDULER_FLAGS),
        name="mixer_prompt",
    )(x, *wts)


def _mixer_sample_kernel(x_ref, h0_ref, cbuf_ref, pbuf_ref,
                         gmix_ref, win_ref, convw_ref, convb_ref, wbd_ref, bra_ref, brx_ref, lam_ref,
                         poolw_ref, pscale_ref, wpa_ref, wpb_ref, wout_ref,
                         x1_ref, h_ref, conv_ref, pool_ref, *, sl):
    sb, d = h0_ref.shape
    xs = [x_ref[:, l, :] for l in range(sl)]
    x = jnp.concatenate(xs, axis=0)
    z = _in_proj(x, gmix_ref[...], win_ref[...])
    rows = lambda v, l: v[l * sb:(l + 1) * sb]
    xa = [rows(z[:, :d], l) for l in range(sl)]
    xb = [rows(z[:, d:2 * d], l) for l in range(sl)]
    ga, gb = z[:, 2 * d:3 * d], z[:, 3 * d:]

    cext = [cbuf_ref[:, k, :] for k in range(CONV_WIDTH - 1)] + xa
    cw = convw_ref[...]
    xcs = []
    for l in range(sl):
        acc = convb_ref[...]
        for k in range(CONV_WIDTH):
            acc = acc + cext[l + k] * cw[k:k + 1]
        xcs.append(acc)
    xc = jnp.concatenate(xcs, axis=0)
    for k in range(CONV_WIDTH - 1):
        conv_ref[:, k, :] = cext[sl + k]

    r, ig = _gates(xc, wbd_ref, bra_ref[...], brx_ref[...])
    a, mult = _lru_coeffs(r, lam_ref[...])
    bterm = mult * ig * xc
    h = h0_ref[...]
    hs = []
    for l in range(sl):
        bl = rows(bterm, l)
        if PAST_LEN + l == 0:
            bl = rows(ig * xc, l)
        h = rows(a, l) * h + bl
        hs.append(h)
    h_ref[...] = h

    gd = d // len(POOL_WINDOWS)
    pext = [pbuf_ref[:, k, :] for k in range(POOL_BUF)] + xb
    for k in range(POOL_BUF):
        pool_ref[:, k, :] = pext[sl + k]
    mixed_rows = []
    for l in range(sl):
        parts = []
        for g, win in enumerate(POOL_WINDOWS):
            sl_g = slice(g * gd, (g + 1) * gd)
            tot = pext[POOL_BUF + l][:, sl_g]
            for jj in range(1, win):
                tot = tot + pext[POOL_BUF + l - jj][:, sl_g]
            cnt = float(min(win, PAST_LEN + l + 1))
            parts.append(tot / cnt - xb[l][:, sl_g])
        mixed_rows.append(jnp.concatenate(parts, axis=-1))
    yb = _pool_proj(jnp.concatenate(mixed_rows, axis=0), poolw_ref, pscale_ref[...])

    x1_ref[...] = _merge_out(x, jnp.concatenate(hs, axis=0), yb, ga, gb,
                             wpa_ref[...], wpb_ref[...], wout_ref[...])


def _mixer_sample(x, h0, cbuf, pbuf, wts, sl):
    _, sb, d = h0.shape
    kern = functools.partial(_mixer_sample_kernel, sl=sl)
    states = (h0, cbuf, pbuf)
    ins = (x,) + states + tuple(wts)
    out_shape = (jax.ShapeDtypeStruct((sl * sb, d), F32),) + tuple(
        jax.ShapeDtypeStruct(v.shape, F32) for v in states)
    full = lambda s: pl.BlockSpec(s, lambda i: (0,) * len(s))
    layer0 = lambda s: pl.BlockSpec((None,) + tuple(s[1:]), lambda i: (0,) * len(s))
    return pl.pallas_call(
        kern,
        grid=(1,),
        in_specs=[full(x.shape)] + [layer0(v.shape) for v in states] + [full(w.shape) for w in wts],
        out_specs=(full(out_shape[0].shape),) + tuple(layer0(v.shape) for v in states),
        out_shape=out_shape,
        compiler_params=pltpu.CompilerParams(
            dimension_semantics=("arbitrary",), vmem_limit_bytes=VMEM_LIMIT),
        name="mixer_sample",
    )(*ins)


def _two_source_specs(tm, width, n_p_tiles):
    return [
        pl.BlockSpec((tm, width), lambda i, *_: (jnp.minimum(i, n_p_tiles - 1), 0)),
        pl.BlockSpec((tm, width), lambda i, *_: (jnp.maximum(i - n_p_tiles, 0), 0)),
    ]


def _router_kernel(xp_ref, xs_ref, g_ref, whi_ref, wlo_ref, br_ref, tri_ref,
                   idx_ref, rank_ref, wtok_ref, cnt_ref, base_ref, *, n_p_tiles):
    i = pl.program_id(0)
    tm = xp_ref.shape[0]

    @pl.when(i == 0)
    def _():
        base_ref[...] = jnp.zeros_like(base_ref)

    x = jnp.where(i < n_p_tiles, xp_ref[...], xs_ref[...])
    u = _rmsnorm(x, g_ref[...])
    u_hi = u.astype(BF16)
    u_lo = (u - u_hi.astype(F32)).astype(BF16)
    logits = _dot(u_hi, whi_ref[...]) + (_dot(u_hi, wlo_ref[...]) + _dot(u_lo, whi_ref[...]))
    lt = (logits + br_ref[...]).T[:N_EXPERTS]

    eio = lax.broadcasted_iota(I32, (N_EXPERTS, tm), 0)
    vals, idxs, sels = [], [], []
    cur = lt
    for _ in range(TOP_K):
        m = jnp.max(cur, axis=0, keepdims=True)
        ik = jnp.min(jnp.where(cur == m, eio, N_EXPERTS), axis=0, keepdims=True)
        sel = eio == ik
        vals.append(m)
        idxs.append(ik)
        sels.append(sel)
        cur = jnp.where(sel, -jnp.inf, cur)
    es = [jnp.exp(v - vals[0]) for v in vals]
    den = es[0]
    for e in es[1:]:
        den = den + e
    ws = [e / den for e in es]

    multi = sels[0].astype(F32)
    for s in sels[1:]:
        multi = multi + s.astype(F32)
    before = _dot(multi.astype(BF16), tri_ref[...]) + base_ref[:, 0:1]
    ranks = [jnp.sum(jnp.where(s, before, 0.0), axis=0, keepdims=True).astype(I32) for s in sels]

    idx_ref[...] = jnp.concatenate(idxs, axis=0)
    rank_ref[...] = jnp.concatenate(ranks, axis=0)
    wpad = jnp.concatenate(ws + [jnp.zeros((V7X_LANES - TOP_K, tm), F32)], axis=0)
    wtok_ref[...] = wpad.T
    new_base = base_ref[...] + jnp.sum(multi, axis=1, keepdims=True)
    base_ref[...] = new_base
    cnt_ref[...] = new_base.astype(I32)


def _router(x1p, x1s, g, whi, wlo, br, tm):
    t_p, d = x1p.shape
    t_s = x1s.shape[0]
    n_p_tiles, n_s_tiles = t_p // tm, t_s // tm
    t = t_p + t_s
    tri = (lax.broadcasted_iota(I32, (tm, tm), 0) < lax.broadcasted_iota(I32, (tm, tm), 1)).astype(BF16)
    kern = functools.partial(_router_kernel, n_p_tiles=n_p_tiles)
    out_shape = (
        jax.ShapeDtypeStruct((TOP_K, t), I32),
        jax.ShapeDtypeStruct((TOP_K, t), I32),
        jax.ShapeDtypeStruct((t, V7X_LANES), F32),
        jax.ShapeDtypeStruct((N_EXPERTS, V7X_LANES), I32),
    )
    consts = (g, whi, wlo, br, tri)
    return pl.pallas_call(
        kern,
        grid=(n_p_tiles + n_s_tiles,),
        in_specs=_two_source_specs(tm, d, n_p_tiles) + [_const_spec(c.shape) for c in consts],
        out_specs=(
            pl.BlockSpec((TOP_K, tm), lambda i: (0, i)),
            pl.BlockSpec((TOP_K, tm), lambda i: (0, i)),
            pl.BlockSpec((tm, V7X_LANES), lambda i: (i, 0)),
            pl.BlockSpec((N_EXPERTS, V7X_LANES), lambda i: (0, 0)),
        ),
        out_shape=out_shape,
        scratch_shapes=[pltpu.VMEM((N_EXPERTS, V7X_LANES), F32)],
        compiler_params=pltpu.CompilerParams(
            dimension_semantics=("arbitrary",), vmem_limit_bytes=VMEM_LIMIT),
        name="router",
    )(x1p, x1s, *consts)


def _row_copy(src, src_row, dst, dst_row, sem):
    return pltpu.make_async_copy(
        src.at[pl.ds(pl.multiple_of(src_row * ROW_CHUNKS, ROW_CHUNKS), ROW_CHUNKS), :],
        dst.at[pl.ds(pl.multiple_of(dst_row * ROW_CHUNKS, ROW_CHUNKS), ROW_CHUNKS), :],
        sem)


def _to_row_tiles(dst_ref, base, val):
    n = val.shape[0]
    for c in range(ROW_CHUNKS):
        dst_ref[pl.ds(base + c, n, stride=ROW_CHUNKS), :] = val[:, c * V7X_LANES:(c + 1) * V7X_LANES]


def _to_row_tiles_range(dst_ref, base, val, tmp_ref, first_row, lo, hi):
    n = val.shape[0]
    _to_row_tiles(tmp_ref, 0, val)
    tok = first_row + lax.shift_right_logical(
        lax.broadcasted_iota(I32, (n * ROW_CHUNKS, V7X_LANES), 0), ROW_CHUNKS.bit_length() - 1)
    mask = jnp.logical_and(tok >= lo, tok < hi)
    pltpu.store(dst_ref.at[pl.ds(base, n * ROW_CHUNKS), :], tmp_ref[...], mask=mask)


def _from_row_tiles(src_ref, base, n):
    return jnp.concatenate(
        [src_ref[pl.ds(base + c, n, stride=ROW_CHUNKS), :] for c in range(ROW_CHUNKS)], axis=-1)


def _issue_rows(n_tokens, issue_token):
    def trip(g, carry):
        for u in range(ISSUE_UNROLL):
            issue_token(g * ISSUE_UNROLL + u)
        return carry

    lax.fori_loop(0, n_tokens // ISSUE_UNROLL, trip, 0)


def _dispatch_kernel(pos_ref, xp_ref, xs_ref, g_ref, out_hbm, slab_ref, sems, *, n_p_tiles, n_tiles):
    i = pl.program_id(0)
    td = xp_ref.shape[0]
    slot = i % 2
    x = jnp.where(i < n_p_tiles, xp_ref[...], xs_ref[...])
    _to_row_tiles(slab_ref, slot * (td * ROW_CHUNKS), _rmsnorm(x, g_ref[...]))

    for t in range(td):
        for k in range(TOP_K):
            _row_copy(slab_ref, slot * td + t, out_hbm, pos_ref[k, t], sems.at[slot]).start(priority=k % 2)

    def drain(s):
        for _ in range(TOP_K):
            pltpu.make_async_copy(slab_ref.at[pl.ds(0, td * ROW_CHUNKS), :],
                                  out_hbm.at[pl.ds(0, td * ROW_CHUNKS), :], sems.at[s]).wait()

    @pl.when(i > 0)
    def _():
        drain(1 - slot)

    @pl.when(i == n_tiles - 1)
    def _():
        drain(slot)


def _dispatch(pos, x1p, x1s, g, td):
    t_p, d = x1p.shape
    t_s = x1s.shape[0]
    n_p_tiles, n_s_tiles = t_p // td, t_s // td
    t = t_p + t_s
    kern = functools.partial(_dispatch_kernel, n_p_tiles=n_p_tiles, n_tiles=n_p_tiles + n_s_tiles)
    return pl.pallas_call(
        kern,
        grid=(n_p_tiles + n_s_tiles,),
        in_specs=[pl.BlockSpec((TOP_K, td), lambda i: (0, i), memory_space=pltpu.SMEM)]
        + _two_source_specs(td, d, n_p_tiles) + [_const_spec(g.shape)],
        out_specs=pl.BlockSpec(memory_space=pl.ANY),
        out_shape=jax.ShapeDtypeStruct((TOP_K * t * ROW_CHUNKS, V7X_LANES), F32),
        scratch_shapes=[pltpu.VMEM((2 * td * ROW_CHUNKS, V7X_LANES), F32), pltpu.SemaphoreType.DMA((2,))],
        compiler_params=pltpu.CompilerParams(
            dimension_semantics=("arbitrary",), vmem_limit_bytes=VMEM_LIMIT),
        name="dispatch",
    )(pos, x1p, x1s, g)


def _experts_kernel(tile_ref, exp_ref, lo_ref, hi_ref, wchg_ref, first_ref, next_ref,
                    xs_ref, wgu_hbm, bgu_ref, wd_hbm, bd_ref, ys_ref,
                    wgu_s, wd_s, wgu_stage, wd_stage, tmp_ref, sems, *, tmx, sub):
    i = pl.program_id(0)
    de = wd_s.shape[0]

    def weight_copies(e):
        return (pltpu.make_async_copy(wgu_hbm.at[e], wgu_stage, sems.at[0]),
                pltpu.make_async_copy(wd_hbm.at[e], wd_stage, sems.at[1]))

    @pl.when(wchg_ref[i] == 1)
    def _():
        @pl.when(i == 0)
        def _():
            for cp in weight_copies(exp_ref[i]):
                cp.start()

        for cp in weight_copies(exp_ref[i]):
            cp.wait()
        wgu_s[...] = wgu_stage[...].astype(BF16)
        wd_s[...] = wd_stage[...].astype(BF16)

        @pl.when(next_ref[i] >= 0)
        def _():
            for cp in weight_copies(next_ref[i]):
                cp.start()

    def mlp(first_row, n):
        x = _from_row_tiles(xs_ref, first_row * ROW_CHUNKS, n).astype(BF16)
        y = None
        for c0 in range(0, de, HIDDEN_CHUNK):
            c1 = c0 + HIDDEN_CHUNK
            gate = _dot(x, wgu_s[:, c0:c1]) + bgu_ref[0, :, c0:c1]
            up = _dot(x, wgu_s[:, de + c0:de + c1]) + bgu_ref[0, :, de + c0:de + c1]
            gate = jnp.minimum(gate, SWIGLU_LIMIT)
            up = jnp.clip(up, -SWIGLU_LIMIT, SWIGLU_LIMIT)
            h = (up + 1.0) * (gate * _sigmoid(SWIGLU_ALPHA * gate))
            part = _dot(h.astype(BF16), wd_s[c0:c1, :])
            y = part if y is None else y + part
        return y + bd_ref[0]

    lo, hi = lo_ref[i], hi_ref[i]
    t0 = tile_ref[i] * tmx
    owns_tile = jnp.logical_and(lo <= t0, hi >= t0 + tmx)

    @pl.when(owns_tile)
    def _():
        for r in range(0, tmx, EXPERT_HALF_ROWS):
            _to_row_tiles(ys_ref, r * ROW_CHUNKS, mlp(r, EXPERT_HALF_ROWS))

    @pl.when(jnp.logical_and(first_ref[i] == 1, jnp.logical_and(hi > lo, jnp.logical_not(owns_tile))))
    def _():
        ys_ref[...] = jnp.zeros_like(ys_ref)

    for sb in range(tmx // sub):
        r0 = t0 + sb * sub
        overlaps = jnp.logical_and(hi > r0, lo < r0 + sub)

        @pl.when(jnp.logical_and(overlaps, jnp.logical_not(owns_tile)))
        def _(sb=sb, r0=r0):
            y = mlp(sb * sub, sub)
            whole = jnp.logical_and(lo <= r0, hi >= r0 + sub)

            @pl.when(whole)
            def _():
                _to_row_tiles(ys_ref, sb * sub * ROW_CHUNKS, y)

            @pl.when(jnp.logical_not(whole))
            def _():
                _to_row_tiles_range(ys_ref, sb * sub * ROW_CHUNKS, y, tmp_ref, r0, lo, hi)


def _experts(plan, xs, wgu, bgu, wd, bd, tmx, sub):
    n_rows = xs.shape[0] // ROW_CHUNKS
    n_work = plan[0].shape[0]
    _, d, de2 = wgu.shape
    de = wd.shape[1]
    kern = functools.partial(_experts_kernel, tmx=tmx, sub=sub)
    grid_spec = pltpu.PrefetchScalarGridSpec(
        num_scalar_prefetch=len(plan),
        grid=(n_work,),
        in_specs=[
            pl.BlockSpec((tmx * ROW_CHUNKS, V7X_LANES), lambda i, tile, *_: (tile[i], 0)),
            pl.BlockSpec(memory_space=pl.ANY),
            pl.BlockSpec((1, 1, de2), lambda i, tile, ex, *_: (ex[i], 0, 0)),
            pl.BlockSpec(memory_space=pl.ANY),
            pl.BlockSpec((1, 1, d), lambda i, tile, ex, *_: (ex[i], 0, 0)),
        ],
        out_specs=pl.BlockSpec((tmx * ROW_CHUNKS, V7X_LANES), lambda i, tile, *_: (tile[i], 0)),
        scratch_shapes=[pltpu.VMEM((d, de2), BF16), pltpu.VMEM((de, d), BF16),
                        pltpu.VMEM((d, de2), F32), pltpu.VMEM((de, d), F32),
                        pltpu.VMEM((sub * ROW_CHUNKS, V7X_LANES), F32),
                        pltpu.SemaphoreType.DMA((2,))],
    )
    return pl.pallas_call(
        kern,
        grid_spec=grid_spec,
        out_shape=jax.ShapeDtypeStruct((n_rows * ROW_CHUNKS, V7X_LANES), F32),
        compiler_params=pltpu.CompilerParams(
            dimension_semantics=("arbitrary",), vmem_limit_bytes=VMEM_LIMIT),
        name="experts",
    )(*plan, xs, wgu, bgu, wd, bd)


def _combine_kernel(pos_ref, posn_ref, ys_hbm, xp_ref, xs_ref, pp_ref, ps_ref, wtok_ref,
                    gple_ref, wgate_ref, wple_ref, gpost_ref, gfin_ref,
                    yp_ref, ysm_ref, gath_ref, ytmp_ref, sems, *, n_p_tiles, n_tiles):
    i = pl.program_id(0)
    tc = xp_ref.shape[0]
    slot = i % 2
    slot_rows = TOP_K * tc

    def gather(p_ref, s):
        def issue_token(t):
            for k in range(TOP_K):
                _row_copy(ys_hbm, p_ref[k, t], gath_ref, s * slot_rows + k * tc + t,
                          sems.at[s]).start(priority=k % 2)

        _issue_rows(tc, issue_token)

    @pl.when(i == 0)
    def _():
        gather(pos_ref, slot)

    is_p = i < n_p_tiles
    base = slot * (slot_rows * ROW_CHUNKS)
    pltpu.make_async_copy(ys_hbm.at[pl.ds(0, slot_rows * ROW_CHUNKS), :],
                          gath_ref.at[pl.ds(base, slot_rows * ROW_CHUNKS), :], sems.at[slot]).wait()

    sub = min(COMBINE_SUB_ROWS, tc)

    def sub_tile(r0):
        for t in range(r0, r0 + sub):
            for k in range(TOP_K):
                _row_copy(ys_hbm, posn_ref[k, t], gath_ref, (1 - slot) * slot_rows + k * tc + t,
                          sems.at[1 - slot]).start(priority=k % 2)
        rows = pl.ds(r0, sub)
        x1 = jnp.where(is_p, xp_ref[rows, :], xs_ref[rows, :])
        p = jnp.where(is_p, pp_ref[rows, :], ps_ref[rows, :])
        ple = _rmsnorm(_dot(p.astype(BF16), wple_ref[...]), gpost_ref[...])
        wt = wtok_ref[rows, :]
        moe = wt[:, 0:1] * _from_row_tiles(gath_ref, base + r0 * ROW_CHUNKS, sub)
        for k in range(1, TOP_K):
            moe = moe + wt[:, k:k + 1] * _from_row_tiles(gath_ref, base + (k * tc + r0) * ROW_CHUNKS, sub)
        x2 = x1 + moe
        gate = _sigmoid(_dot(_rmsnorm(x2, gple_ref[...]).astype(BF16), wgate_ref[...]))
        ytmp_ref[rows, :] = _rmsnorm(x2 + ple * gate, gfin_ref[...])

    for r0 in range(0, tc, sub):
        sub_tile(r0)

    @pl.when(is_p)
    def _():
        yp_ref[...] = ytmp_ref[...]

    @pl.when(jnp.logical_not(is_p))
    def _():
        ysm_ref[...] = ytmp_ref[...]

    @pl.when(i == n_tiles - 1)
    def _():
        spare = (1 - slot) * (slot_rows * ROW_CHUNKS)
        pltpu.make_async_copy(ys_hbm.at[pl.ds(0, slot_rows * ROW_CHUNKS), :],
                              gath_ref.at[pl.ds(spare, slot_rows * ROW_CHUNKS), :], sems.at[1 - slot]).wait()


def _combine(pos, ys, x1p, x1s, pp, ps, wtok, consts, tc):
    t_p, d = x1p.shape
    t_s = x1s.shape[0]
    n_p_tiles, n_s_tiles = t_p // tc, t_s // tc
    n_tiles = n_p_tiles + n_s_tiles
    kern = functools.partial(_combine_kernel, n_p_tiles=n_p_tiles, n_tiles=n_tiles)
    return pl.pallas_call(
        kern,
        grid=(n_tiles,),
        in_specs=[pl.BlockSpec((TOP_K, tc), lambda i: (0, i), memory_space=pltpu.SMEM),
                  pl.BlockSpec((TOP_K, tc), lambda i: (0, jnp.minimum(i + 1, n_tiles - 1)),
                               memory_space=pltpu.SMEM),
                  pl.BlockSpec(memory_space=pl.ANY)]
        + _two_source_specs(tc, d, n_p_tiles) + _two_source_specs(tc, pp.shape[1], n_p_tiles)
        + [pl.BlockSpec((tc, V7X_LANES), lambda i: (i, 0))] + [_const_spec(c.shape) for c in consts],
        out_specs=tuple(_two_source_specs(tc, d, n_p_tiles)),
        out_shape=(jax.ShapeDtypeStruct((t_p, d), F32), jax.ShapeDtypeStruct((t_s, d), F32)),
        scratch_shapes=[pltpu.VMEM((2 * TOP_K * tc * ROW_CHUNKS, V7X_LANES), F32),
                        pltpu.VMEM((tc, d), F32),
                        pltpu.SemaphoreType.DMA((2,))],
        compiler_params=pltpu.CompilerParams(
            dimension_semantics=("arbitrary",), vmem_limit_bytes=VMEM_LIMIT),
        name="combine",
    )(pos, pos, ys, x1p, x1s, pp, ps, wtok, *consts)


def _plan(idx, rank, counts, tmx, n_work):
    eids = jnp.arange(N_EXPERTS, dtype=I32)
    incl = eids[None, :] <= eids[:, None]
    ends = jnp.sum(jnp.where(incl, counts[None, :], 0), axis=1)
    offs = ends - counts
    pos = rank + jnp.sum(jnp.where(idx[..., None] == eids, offs, 0), axis=-1)

    first_tile = offs // tmx
    last_tile = (ends - 1) // tmx
    n_e = jnp.where(counts > 0, last_tile - first_tile + 1, 0)
    iend = jnp.sum(jnp.where(incl, n_e[None, :], 0), axis=1)
    istart = iend - n_e
    total = iend[N_EXPERTS - 1]
    i = jnp.arange(n_work, dtype=I32)
    ic = jnp.minimum(i, total - 1)
    e_i = jnp.sum((ic[:, None] >= iend[None, :]).astype(I32), axis=1)
    onehot = e_i[:, None] == eids[None, :]
    pick = lambda v: jnp.sum(jnp.where(onehot, v[None, :], 0), axis=1)
    tile_i = pick(first_tile) + (ic - pick(istart))
    valid = i < total
    lo = jnp.where(valid, jnp.maximum(pick(offs), tile_i * tmx), 0)
    hi = jnp.where(valid, jnp.minimum(pick(ends), (tile_i + 1) * tmx), 0)
    prev_e = jnp.concatenate([jnp.full((1,), -1, I32), e_i[:-1]])
    prev_tile = jnp.concatenate([jnp.full((1,), -1, I32), tile_i[:-1]])
    wchg = (e_i != prev_e).astype(I32)
    first = (tile_i != prev_tile).astype(I32)
    later = jnp.logical_and(eids[None, :] > e_i[:, None], counts[None, :] > 0)
    nxt = jnp.min(jnp.where(later, eids[None, :], N_EXPERTS), axis=1)
    nxt = jnp.where(nxt < N_EXPERTS, nxt, -1)
    return pos.astype(I32), tuple(v.astype(I32) for v in (tile_i, e_i, lo, hi, wchg, first, nxt))


def _block_diag_gates(w_a, w_x):
    heads, hd, _ = w_a.shape
    per = GATE_GROUP // hd
    groups = heads // per
    eye = jnp.eye(per, dtype=w_a.dtype)

    def bd(w):
        w4 = w.reshape(groups, per, hd, hd)
        return jnp.einsum('ghij,hk->ghikj', w4, eye).reshape(groups, GATE_GROUP, GATE_GROUP)

    return jnp.concatenate([bd(w_a), bd(w_x)], axis=-1).astype(BF16)


def _pick_tile(pref, *sizes):
    t = pref
    while any(s % t for s in sizes):
        t //= 2
    return t


def kernel(x_prompt, x_sample, p_prompt, p_sample, state_lru_h, state_conv, state_pool, g_mix, w_in, conv_w, conv_b, w_rg_a, b_rg_a, w_rg_x, b_rg_x, lru_lambda, pool_w, pool_scale, w_proj_a, w_proj_b, w_out, g_moe, w_router, b_router, w_gate_up, b_gate_up, w_down, b_down, g_ple, w_ple_gate, w_ple, g_ple_post, g_final):
    depth = g_mix.shape[0]
    assert depth == 1, "single-layer trunk"
    nb, seq, d = x_prompt.shape
    sb, sl, _ = x_sample.shape
    t_p, t_s = nb * seq, sb * sl
    t = t_p + t_s
    row = lambda v: v.reshape(1, -1)

    mixer_wts = (
        row(g_mix[0]), w_in[0].astype(BF16), conv_w[0], row(conv_b[0]),
        _block_diag_gates(w_rg_a[0], w_rg_x[0]), row(b_rg_a[0]), row(b_rg_x[0]), row(lru_lambda[0]),
        pool_w[0].astype(BF16), row(pool_scale[0]),
        w_proj_a[0].astype(BF16), w_proj_b[0].astype(BF16), w_out[0].astype(BF16),
    )

    tl = _pick_tile(256, seq)
    x1p, h_p, conv_p, pool_p = _mixer_prompt(x_prompt, mixer_wts, tl)
    x1p = x1p.reshape(t_p, d)
    x1s, h_s, conv_s, pool_s = _mixer_sample(
        x_sample, state_lru_h, state_conv, state_pool, mixer_wts, sl)

    wr = jnp.pad(w_router[0], ((0, 0), (0, V7X_LANES - N_EXPERTS)))
    wr_hi = wr.astype(BF16)
    wr_lo = (wr - wr_hi.astype(F32)).astype(BF16)
    br = jnp.pad(b_router[0], (0, V7X_LANES - N_EXPERTS)).reshape(1, -1)
    tm = _pick_tile(512, t_p, t_s)
    idx, rank, wtok, cnt = _router(x1p, x1s, row(g_moe[0]), wr_hi, wr_lo, br, tm)

    tmx = _pick_tile(512, TOP_K * t)
    n_work = (TOP_K * t) // tmx + N_EXPERTS - 1
    pos, plan = _plan(idx, rank, cnt[:, 0], tmx, n_work)

    xs = _dispatch(pos, x1p, x1s, row(g_moe[0]), tm)
    ys = _experts(plan, xs, w_gate_up[0], b_gate_up[0][:, None, :], w_down[0], b_down[0][:, None, :], tmx,
                  _pick_tile(EXPERT_SUB_ROWS, tmx))

    pp = p_prompt[0].reshape(t_p, -1)
    ps = jnp.swapaxes(p_sample[0], 0, 1).reshape(t_s, -1)
    tc = _pick_tile(512, t_p, t_s)
    consts = (row(g_ple[0]), w_ple_gate[0].astype(BF16), w_ple[0].astype(BF16), row(g_ple_post[0]), row(g_final))
    y_p, y_s = _combine(pos, ys, x1p, x1s, pp, ps, wtok, consts, tc)

    y_prompt = y_p.reshape(nb, seq, d)
    y_sample = jnp.swapaxes(y_s.reshape(sl, sb, d), 0, 1)
    return (y_prompt, y_sample,
            h_p.reshape(depth, nb, d), conv_p[None], pool_p[None],
            h_s, conv_s, pool_s)
```

```python
import functools

import jax
import jax.numpy as jnp
from jax import lax
from jax.experimental import pallas as pl
from jax.experimental.pallas import tpu as pltpu

F32 = jnp.float32
BF16 = jnp.bfloat16
I32 = jnp.int32

EPS = 1e-6
LRU_C = 8.0
LRU_HEADS = 16
CONV_WIDTH = 4
POOL_WINDOWS = (2, 4, 8, 16)
POOL_BUF = max(POOL_WINDOWS) - 1
N_EXPERTS = 32
TOP_K = 4
SWIGLU_LIMIT = 7.0
SWIGLU_ALPHA = 1.702
PAST_LEN = 16384

V7X_LANES = 128
V7X_SUBLANES = 8
V7X_VMEM_BYTES = 64 * 1024 * 1024
VMEM_LIMIT = V7X_VMEM_BYTES - 8 * 1024 * 1024

EXPERT_SUB_ROWS = 128
EXPERT_HALF_ROWS = 512
HIDDEN_CHUNK = 512
ISSUE_UNROLL = 8
COMBINE_SUB_ROWS = 128
MIXER_SEQS_PER_STEP = 2
MIXER_STAGE_LAG = 5
MIXER_SCHEDULER_FLAGS = None
GATE_GROUP = 256
ROW_CHUNKS = 8


def _rmsnorm(x, g):
    ms = jnp.mean(x * x, axis=-1, keepdims=True)
    return x * lax.rsqrt(ms + EPS) * g


def _sigmoid(x):
    return 1.0 / (1.0 + jnp.exp(-x))


def _softplus(x):
    return jnp.maximum(x, 0.0) + jnp.log1p(jnp.exp(-jnp.abs(x)))


def _dot(a, b):
    return jnp.dot(a, b, preferred_element_type=F32)


def _in_proj(x, g, w_in):
    return _dot(_rmsnorm(x, g).astype(BF16), w_in)


def _gates(xc, wbd_ref, bra, brx):
    xcb = xc.astype(BF16)
    n_groups = xc.shape[1] // GATE_GROUP
    rs, gs = [], []
    for g in range(n_groups):
        o = _dot(xcb[:, g * GATE_GROUP:(g + 1) * GATE_GROUP], wbd_ref[g])
        rs.append(o[:, :GATE_GROUP])
        gs.append(o[:, GATE_GROUP:])
    r = _sigmoid(jnp.concatenate(rs, axis=-1) + bra)
    ig = _sigmoid(jnp.concatenate(gs, axis=-1) + brx)
    return r, ig


def _lru_coeffs(r, lam):
    log_a = (-LRU_C * _softplus(-lam)) * r
    a = jnp.exp(log_a)
    th = jnp.tanh(log_a)
    q = -2.0 * th
    mult = jnp.where(q > 0.0, q * lax.rsqrt(q * (1.0 - th)), 0.0)
    return a, mult


def _scan_rows(a, b, h0):
    tl, w = a.shape
    groups = tl // V7X_SUBLANES
    a3 = a.reshape(groups, V7X_SUBLANES, w)
    b3 = b.reshape(groups, V7X_SUBLANES, w)
    sub = lax.broadcasted_iota(I32, (groups, V7X_SUBLANES, w), 1)
    for s in (1, 2, 4):
        a_sh = pltpu.roll(a3, s, 1)
        b_sh = pltpu.roll(b3, s, 1)
        valid = sub >= s
        b3 = jnp.where(valid, a3 * b_sh + b3, b3)
        a3 = jnp.where(valid, a3 * a_sh, a3)
    hs = []
    h = h0
    for g in range(groups):
        hg = a3[g] * h + b3[g]
        hs.append(hg)
        h = hg[V7X_SUBLANES - 1:V7X_SUBLANES, :]
    return jnp.concatenate(hs, axis=0), h


def _merge_out(x, h, yb, ga, gb, wpa, wpb, wout):
    pa = _dot(h.astype(BF16), wpa)
    pb = _dot(yb.astype(BF16), wpb)
    merged = _sigmoid(ga) * pa + _sigmoid(gb) * pb
    return x + _dot(merged.astype(BF16), wout)


def _pool_proj(mixed, poolw_ref, pscale):
    gd = mixed.shape[1] // len(POOL_WINDOWS)
    mb = mixed.astype(BF16)
    outs = [_dot(mb[:, g * gd:(g + 1) * gd], poolw_ref[g]) for g in range(len(POOL_WINDOWS))]
    return jnp.concatenate(outs, axis=-1) * pscale


def _mixer_prompt_kernel(x_ref, gmix_ref, win_ref, convw_ref, convb_ref, wbd_ref, bra_ref, brx_ref, lam_ref,
                         poolw_ref, pscale_ref, wpa_ref, wpb_ref, wout_ref,
                         x1_ref, h_ref, conv_ref, pool_ref,
                         hc_ref, cc_ref, pc_ref, *, tl, n_tiles):
    j = pl.program_id(1)
    d = x_ref.shape[-1]

    @pl.when(j == 0)
    def _():
        hc_ref[...] = jnp.zeros_like(hc_ref)
        cc_ref[...] = jnp.zeros_like(cc_ref)
        pc_ref[...] = jnp.zeros_like(pc_ref)

    tiles = [_mixer_prompt_tile(j, x_ref.at[q], gmix_ref, win_ref, convw_ref, convb_ref, wbd_ref, bra_ref,
                                brx_ref, lam_ref, poolw_ref, pscale_ref, wpa_ref, wpb_ref, wout_ref,
                                x1_ref.at[q], h_ref.at[q], conv_ref.at[q], pool_ref.at[q],
                                hc_ref.at[q], cc_ref.at[q], pc_ref.at[q], tl=tl, n_tiles=n_tiles)
             for q in range(x_ref.shape[0])]
    _interleave(tiles, MIXER_STAGE_LAG)


def _interleave(stage_iters, lag):
    live = dict(enumerate(stage_iters))
    step = 0
    while live:
        for q in sorted(live):
            if step >= q * lag:
                try:
                    next(live[q])
                except StopIteration:
                    del live[q]
        step += 1


def _mixer_prompt_tile(j, x_ref, gmix_ref, win_ref, convw_ref, convb_ref, wbd_ref, bra_ref, brx_ref, lam_ref,
                       poolw_ref, pscale_ref, wpa_ref, wpb_ref, wout_ref,
                       x1_ref, h_ref, conv_ref, pool_ref,
                       hc_ref, cc_ref, pc_ref, *, tl, n_tiles):
    d = x_ref.shape[-1]
    x = x_ref[...]
    u = _rmsnorm(x, gmix_ref[...]).astype(BF16)
    yield
    xa = _dot(u, win_ref[:, 0:d])
    yield
    xb = _dot(u, win_ref[:, d:2 * d])
    yield
    ga = _dot(u, win_ref[:, 2 * d:3 * d])
    yield
    gb = _dot(u, win_ref[:, 3 * d:4 * d])
    yield
    row = lax.broadcasted_iota(I32, (tl, 1), 0)

    full = jnp.concatenate([cc_ref[...], xa], axis=0)
    cw = convw_ref[...]
    xc = convb_ref[...]
    for k in range(CONV_WIDTH):
        s = CONV_WIDTH - 1 - k
        term = xa if s == 0 else pltpu.roll(full, s, 0)[V7X_SUBLANES:]
        xc = xc + term * cw[k:k + 1]
    yield

    r, ig = _gates(xc, wbd_ref, bra_ref[...], brx_ref[...])
    yield
    a, mult = _lru_coeffs(r, lam_ref[...])
    mult = jnp.where(jnp.logical_and(j == 0, row == 0), 1.0, mult)
    bterm = mult * ig * xc
    yield
    h, h_last = _scan_rows(a, bterm, hc_ref[0:1, :])
    yield

    gd = d // len(POOL_WINDOWS)
    ext = jnp.concatenate([pc_ref[...], xb], axis=0)
    pad = pc_ref.shape[0]
    s2 = ext + pltpu.roll(ext, 1, 0)
    s4 = s2[:, gd:] + pltpu.roll(s2[:, gd:], 2, 0)
    s8 = s4[:, gd:] + pltpu.roll(s4[:, gd:], 4, 0)
    s16 = s8[:, gd:] + pltpu.roll(s8[:, gd:], 8, 0)
    tots = [s2[pad:, :gd], s4[pad:, :gd], s8[pad:, :gd], s16[pad:, :]]
    pos = j * tl + row
    mixed = []
    for g, win in enumerate(POOL_WINDOWS):
        inv = 1.0 / jnp.minimum(win, pos + 1).astype(F32)
        mixed.append(tots[g] * inv - xb[:, g * gd:(g + 1) * gd])
    yield
    yb = _pool_proj(jnp.concatenate(mixed, axis=-1), poolw_ref, pscale_ref[...])
    sga = _sigmoid(ga)
    sgb = _sigmoid(gb)
    yield
    pa = _dot(h.astype(BF16), wpa_ref[...])
    yield
    pb = _dot(yb.astype(BF16), wpb_ref[...])
    merged = (sga * pa + sgb * pb).astype(BF16)
    yield
    x1_ref[...] = x + _dot(merged, wout_ref[...])

    hc_ref[...] = jnp.broadcast_to(h_last, hc_ref.shape)
    cc_ref[...] = xa[tl - cc_ref.shape[0]:, :]
    pc_ref[...] = xb[tl - pc_ref.shape[0]:, :]

    @pl.when(j == n_tiles - 1)
    def _():
        h_ref[...] = h_last
        conv_ref[...] = xa[tl - (CONV_WIDTH - 1):, :]
        pool_ref[...] = xb[tl - POOL_BUF:, :]


def _const_spec(shape):
    nd = len(shape)
    return pl.BlockSpec(shape, lambda *_: (0,) * nd, pipeline_mode=pl.Buffered(1))


def _mixer_weight_specs(wts):
    return [_const_spec(w.shape) for w in wts]


def _mixer_prompt(x, wts, tl):
    nb, seq, d = x.shape
    n_tiles = seq // tl
    nq = _pick_tile(MIXER_SEQS_PER_STEP, nb)
    kern = functools.partial(_mixer_prompt_kernel, tl=tl, n_tiles=n_tiles)
    out_shape = (
        jax.ShapeDtypeStruct((nb, seq, d), F32),
        jax.ShapeDtypeStruct((nb, 1, d), F32),
        jax.ShapeDtypeStruct((nb, CONV_WIDTH - 1, d), F32),
        jax.ShapeDtypeStruct((nb, POOL_BUF, d), F32),
    )
    return pl.pallas_call(
        kern,
        grid=(nb // nq, n_tiles),
        in_specs=[pl.BlockSpec((nq, tl, d), lambda b, j: (b, j, 0))] + _mixer_weight_specs(wts),
        out_specs=(
            pl.BlockSpec((nq, tl, d), lambda b, j: (b, j, 0)),
            pl.BlockSpec((nq, 1, d), lambda b, j: (b, 0, 0)),
            pl.BlockSpec((nq, CONV_WIDTH - 1, d), lambda b, j: (b, 0, 0)),
            pl.BlockSpec((nq, POOL_BUF, d), lambda b, j: (b, 0, 0)),
        ),
        out_shape=out_shape,
        scratch_shapes=[
            pltpu.VMEM((nq, V7X_SUBLANES, d), F32),
            pltpu.VMEM((nq, V7X_SUBLANES, d), F32),
            pltpu.VMEM((nq, 2 * V7X_SUBLANES, d), F32),
        ],
        compiler_params=pltpu.CompilerParams(
            dimension_semantics=("arbitrary", "arbitrary"), vmem_limit_bytes=VMEM_LIMIT,
            flags=MIXER_SCHEDULER_FLAGS),
        name="mixer_prompt",
    )(x, *wts)


def _mixer_sample_kernel(x_ref, h0_ref, cbuf_ref, pbuf_ref,
                         gmix_ref, win_ref, convw_ref, convb_ref, wbd_ref, bra_ref, brx_ref, lam_ref,
                         poolw_ref, pscale_ref, wpa_ref, wpb_ref, wout_ref,
                         x1_ref, h_ref, conv_ref, pool_ref, *, sl):
    sb, d = h0_ref.shape
    xs = [x_ref[:, l, :] for l in range(sl)]
    x = jnp.concatenate(xs, axis=0)
    z = _in_proj(x, gmix_ref[...], win_ref[...])
    rows = lambda v, l: v[l * sb:(l + 1) * sb]
    xa = [rows(z[:, :d], l) for l in range(sl)]
    xb = [rows(z[:, d:2 * d], l) for l in range(sl)]
    ga, gb = z[:, 2 * d:3 * d], z[:, 3 * d:]

    cext = [cbuf_ref[:, k, :] for k in range(CONV_WIDTH - 1)] + xa
    cw = convw_ref[...]
    xcs = []
    for l in range(sl):
        acc = convb_ref[...]
        for k in range(CONV_WIDTH):
            acc = acc + cext[l + k] * cw[k:k + 1]
        xcs.append(acc)
    xc = jnp.concatenate(xcs, axis=0)
    for k in range(CONV_WIDTH - 1):
        conv_ref[:, k, :] = cext[sl + k]

    r, ig = _gates(xc, wbd_ref, bra_ref[...], brx_ref[...])
    a, mult = _lru_coeffs(r, lam_ref[...])
    bterm = mult * ig * xc
    h = h0_ref[...]
    hs = []
    for l in range(sl):
        bl = rows(bterm, l)
        if PAST_LEN + l == 0:
            bl = rows(ig * xc, l)
        h = rows(a, l) * h + bl
        hs.append(h)
    h_ref[...] = h

    gd = d // len(POOL_WINDOWS)
    pext = [pbuf_ref[:, k, :] for k in range(POOL_BUF)] + xb
    for k in range(POOL_BUF):
        pool_ref[:, k, :] = pext[sl + k]
    mixed_rows = []
    for l in range(sl):
        parts = []
        for g, win in enumerate(POOL_WINDOWS):
            sl_g = slice(g * gd, (g + 1) * gd)
            tot = pext[POOL_BUF + l][:, sl_g]
            for jj in range(1, win):
                tot = tot + pext[POOL_BUF + l - jj][:, sl_g]
            cnt = float(min(win, PAST_LEN + l + 1))
            parts.append(tot / cnt - xb[l][:, sl_g])
        mixed_rows.append(jnp.concatenate(parts, axis=-1))
    yb = _pool_proj(jnp.concatenate(mixed_rows, axis=0), poolw_ref, pscale_ref[...])

    x1_ref[...] = _merge_out(x, jnp.concatenate(hs, axis=0), yb, ga, gb,
                             wpa_ref[...], wpb_ref[...], wout_ref[...])


def _mixer_sample(x, h0, cbuf, pbuf, wts, sl):
    _, sb, d = h0.shape
    kern = functools.partial(_mixer_sample_kernel, sl=sl)
    states = (h0, cbuf, pbuf)
    ins = (x,) + states + tuple(wts)
    out_shape = (jax.ShapeDtypeStruct((sl * sb, d), F32),) + tuple(
        jax.ShapeDtypeStruct(v.shape, F32) for v in states)
    full = lambda s: pl.BlockSpec(s, lambda i: (0,) * len(s))
    layer0 = lambda s: pl.BlockSpec((None,) + tuple(s[1:]), lambda i: (0,) * len(s))
    return pl.pallas_call(
        kern,
        grid=(1,),
        in_specs=[full(x.shape)] + [layer0(v.shape) for v in states] + [full(w.shape) for w in wts],
        out_specs=(full(out_shape[0].shape),) + tuple(layer0(v.shape) for v in states),
        out_shape=out_shape,
        compiler_params=pltpu.CompilerParams(
            dimension_semantics=("arbitrary",), vmem_limit_bytes=VMEM_LIMIT),
        name="mixer_sample",
    )(*ins)


def _two_source_specs(tm, width, n_p_tiles):
    return [
        pl.BlockSpec((tm, width), lambda i, *_: (jnp.minimum(i, n_p_tiles - 1), 0)),
        pl.BlockSpec((tm, width), lambda i, *_: (jnp.maximum(i - n_p_tiles, 0), 0)),
    ]


def _router_kernel(xp_ref, xs_ref, g_ref, whi_ref, wlo_ref, br_ref, tri_ref,
                   idx_ref, rank_ref, wtok_ref, cnt_ref, base_ref, *, n_p_tiles):
    i = pl.program_id(0)
    tm = xp_ref.shape[0]

    @pl.when(i == 0)
    def _():
        base_ref[...] = jnp.zeros_like(base_ref)

    x = jnp.where(i < n_p_tiles, xp_ref[...], xs_ref[...])
    u = _rmsnorm(x, g_ref[...])
    u_hi = u.astype(BF16)
    u_lo = (u - u_hi.astype(F32)).astype(BF16)
    logits = _dot(u_hi, whi_ref[...]) + (_dot(u_hi, wlo_ref[...]) + _dot(u_lo, whi_ref[...]))
    lt = (logits + br_ref[...]).T[:N_EXPERTS]

    eio = lax.broadcasted_iota(I32, (N_EXPERTS, tm), 0)
    vals, idxs, sels = [], [], []
    cur = lt
    for _ in range(TOP_K):
        m = jnp.max(cur, axis=0, keepdims=True)
        ik = jnp.min(jnp.where(cur == m, eio, N_EXPERTS), axis=0, keepdims=True)
        sel = eio == ik
        vals.append(m)
        idxs.append(ik)
        sels.append(sel)
        cur = jnp.where(sel, -jnp.inf, cur)
    es = [jnp.exp(v - vals[0]) for v in vals]
    den = es[0]
    for e in es[1:]:
        den = den + e
    ws = [e / den for e in es]

    multi = sels[0].astype(F32)
    for s in sels[1:]:
        multi = multi + s.astype(F32)
    before = _dot(multi.astype(BF16), tri_ref[...]) + base_ref[:, 0:1]
    ranks = [jnp.sum(jnp.where(s, before, 0.0), axis=0, keepdims=True).astype(I32) for s in sels]

    idx_ref[...] = jnp.concatenate(idxs, axis=0)
    rank_ref[...] = jnp.concatenate(ranks, axis=0)
    wpad = jnp.concatenate(ws + [jnp.zeros((V7X_LANES - TOP_K, tm), F32)], axis=0)
    wtok_ref[...] = wpad.T
    new_base = base_ref[...] + jnp.sum(multi, axis=1, keepdims=True)
    base_ref[...] = new_base
    cnt_ref[...] = new_base.astype(I32)


def _router(x1p, x1s, g, whi, wlo, br, tm):
    t_p, d = x1p.shape
    t_s = x1s.shape[0]
    n_p_tiles, n_s_tiles = t_p // tm, t_s // tm
    t = t_p + t_s
    tri = (lax.broadcasted_iota(I32, (tm, tm), 0) < lax.broadcasted_iota(I32, (tm, tm), 1)).astype(BF16)
    kern = functools.partial(_router_kernel, n_p_tiles=n_p_tiles)
    out_shape = (
        jax.ShapeDtypeStruct((TOP_K, t), I32),
        jax.ShapeDtypeStruct((TOP_K, t), I32),
        jax.ShapeDtypeStruct((t, V7X_LANES), F32),
        jax.ShapeDtypeStruct((N_EXPERTS, V7X_LANES), I32),
    )
    consts = (g, whi, wlo, br, tri)
    return pl.pallas_call(
        kern,
        grid=(n_p_tiles + n_s_tiles,),
        in_specs=_two_source_specs(tm, d, n_p_tiles) + [_const_spec(c.shape) for c in consts],
        out_specs=(
            pl.BlockSpec((TOP_K, tm), lambda i: (0, i)),
            pl.BlockSpec((TOP_K, tm), lambda i: (0, i)),
            pl.BlockSpec((tm, V7X_LANES), lambda i: (i, 0)),
            pl.BlockSpec((N_EXPERTS, V7X_LANES), lambda i: (0, 0)),
        ),
        out_shape=out_shape,
        scratch_shapes=[pltpu.VMEM((N_EXPERTS, V7X_LANES), F32)],
        compiler_params=pltpu.CompilerParams(
            dimension_semantics=("arbitrary",), vmem_limit_bytes=VMEM_LIMIT),
        name="router",
    )(x1p, x1s, *consts)


def _row_copy(src, src_row, dst, dst_row, sem):
    return pltpu.make_async_copy(
        src.at[pl.ds(pl.multiple_of(src_row * ROW_CHUNKS, ROW_CHUNKS), ROW_CHUNKS), :],
        dst.at[pl.ds(pl.multiple_of(dst_row * ROW_CHUNKS, ROW_CHUNKS), ROW_CHUNKS), :],
        sem)


def _to_row_tiles(dst_ref, base, val):
    n = val.shape[0]
    for c in range(ROW_CHUNKS):
        dst_ref[pl.ds(base + c, n, stride=ROW_CHUNKS), :] = val[:, c * V7X_LANES:(c + 1) * V7X_LANES]


def _to_row_tiles_range(dst_ref, base, val, tmp_ref, first_row, lo, hi):
    n = val.shape[0]
    _to_row_tiles(tmp_ref, 0, val)
    tok = first_row + lax.shift_right_logical(
        lax.broadcasted_iota(I32, (n * ROW_CHUNKS, V7X_LANES), 0), ROW_CHUNKS.bit_length() - 1)
    mask = jnp.logical_and(tok >= lo, tok < hi)
    pltpu.store(dst_ref.at[pl.ds(base, n * ROW_CHUNKS), :], tmp_ref[...], mask=mask)


def _from_row_tiles(src_ref, base, n):
    return jnp.concatenate(
        [src_ref[pl.ds(base + c, n, stride=ROW_CHUNKS), :] for c in range(ROW_CHUNKS)], axis=-1)


def _issue_rows(n_tokens, issue_token):
    def trip(g, carry):
        for u in range(ISSUE_UNROLL):
            issue_token(g * ISSUE_UNROLL + u)
        return carry

    lax.fori_loop(0, n_tokens // ISSUE_UNROLL, trip, 0)


def _dispatch_kernel(pos_ref, xp_ref, xs_ref, pp_ref, ps_ref, g_ref, wple_ref, gpost_ref,
                     out_hbm, plep_ref, ples_ref, slab_ref, sems, *, n_p_tiles, n_tiles):
    i = pl.program_id(0)
    td = xp_ref.shape[0]
    slot = i % 2
    is_p = i < n_p_tiles
    x = jnp.where(is_p, xp_ref[...], xs_ref[...])
    _to_row_tiles(slab_ref, slot * (td * ROW_CHUNKS), _rmsnorm(x, g_ref[...]))

    p = jnp.where(is_p, pp_ref[...], ps_ref[...])
    ple = _rmsnorm(_dot(p.astype(BF16), wple_ref[...]), gpost_ref[...])

    @pl.when(is_p)
    def _():
        plep_ref[...] = ple

    @pl.when(jnp.logical_not(is_p))
    def _():
        ples_ref[...] = ple

    for t in range(td):
        for k in range(TOP_K):
            _row_copy(slab_ref, slot * td + t, out_hbm, pos_ref[k, t], sems.at[slot]).start(priority=k % 2)

    def drain(s):
        for _ in range(TOP_K):
            pltpu.make_async_copy(slab_ref.at[pl.ds(0, td * ROW_CHUNKS), :],
                                  out_hbm.at[pl.ds(0, td * ROW_CHUNKS), :], sems.at[s]).wait()

    @pl.when(i > 0)
    def _():
        drain(1 - slot)

    @pl.when(i == n_tiles - 1)
    def _():
        drain(slot)


def _dispatch(pos, x1p, x1s, pp, ps, g, wple, gpost, td):
    t_p, d = x1p.shape
    t_s = x1s.shape[0]
    n_p_tiles, n_s_tiles = t_p // td, t_s // td
    t = t_p + t_s
    kern = functools.partial(_dispatch_kernel, n_p_tiles=n_p_tiles, n_tiles=n_p_tiles + n_s_tiles)
    consts = (g, wple, gpost)
    return pl.pallas_call(
        kern,
        grid=(n_p_tiles + n_s_tiles,),
        in_specs=[pl.BlockSpec((TOP_K, td), lambda i: (0, i), memory_space=pltpu.SMEM)]
        + _two_source_specs(td, d, n_p_tiles) + _two_source_specs(td, pp.shape[1], n_p_tiles)
        + [_const_spec(c.shape) for c in consts],
        out_specs=(pl.BlockSpec(memory_space=pl.ANY),) + tuple(_two_source_specs(td, d, n_p_tiles)),
        out_shape=(jax.ShapeDtypeStruct((TOP_K * t * ROW_CHUNKS, V7X_LANES), F32),
                   jax.ShapeDtypeStruct((t_p, d), F32), jax.ShapeDtypeStruct((t_s, d), F32)),
        scratch_shapes=[pltpu.VMEM((2 * td * ROW_CHUNKS, V7X_LANES), F32), pltpu.SemaphoreType.DMA((2,))],
        compiler_params=pltpu.CompilerParams(
            dimension_semantics=("arbitrary",), vmem_limit_bytes=VMEM_LIMIT),
        name="dispatch",
    )(pos, x1p, x1s, pp, ps, *consts)


def _experts_kernel(tile_ref, exp_ref, lo_ref, hi_ref, wchg_ref, first_ref, next_ref,
                    xs_ref, wgu_hbm, bgu_ref, wd_hbm, bd_ref, ys_ref,
                    wgu_s, wd_s, wgu_stage, wd_stage, tmp_ref, sems, *, tmx, sub):
    i = pl.program_id(0)
    de = wd_s.shape[0]

    def weight_copies(e):
        return (pltpu.make_async_copy(wgu_hbm.at[e], wgu_stage, sems.at[0]),
                pltpu.make_async_copy(wd_hbm.at[e], wd_stage, sems.at[1]))

    @pl.when(wchg_ref[i] == 1)
    def _():
        @pl.when(i == 0)
        def _():
            for cp in weight_copies(exp_ref[i]):
                cp.start()

        for cp in weight_copies(exp_ref[i]):
            cp.wait()
        wgu_s[...] = wgu_stage[...].astype(BF16)
        wd_s[...] = wd_stage[...].astype(BF16)

        @pl.when(next_ref[i] >= 0)
        def _():
            for cp in weight_copies(next_ref[i]):
                cp.start()

    def mlp(first_row, n):
        x = _from_row_tiles(xs_ref, first_row * ROW_CHUNKS, n).astype(BF16)
        y = None
        for c0 in range(0, de, HIDDEN_CHUNK):
            c1 = c0 + HIDDEN_CHUNK
            gate = _dot(x, wgu_s[:, c0:c1]) + bgu_ref[0, :, c0:c1]
            up = _dot(x, wgu_s[:, de + c0:de + c1]) + bgu_ref[0, :, de + c0:de + c1]
            gate = jnp.minimum(gate, SWIGLU_LIMIT)
            up = jnp.clip(up, -SWIGLU_LIMIT, SWIGLU_LIMIT)
            h = (up + 1.0) * (gate * _sigmoid(SWIGLU_ALPHA * gate))
            part = _dot(h.astype(BF16), wd_s[c0:c1, :])
            y = part if y is None else y + part
        return y + bd_ref[0]

    lo, hi = lo_ref[i], hi_ref[i]
    t0 = tile_ref[i] * tmx
    owns_tile = jnp.logical_and(lo <= t0, hi >= t0 + tmx)

    @pl.when(owns_tile)
    def _():
        for r in range(0, tmx, EXPERT_HALF_ROWS):
            _to_row_tiles(ys_ref, r * ROW_CHUNKS, mlp(r, EXPERT_HALF_ROWS))

    @pl.when(jnp.logical_and(first_ref[i] == 1, jnp.logical_and(hi > lo, jnp.logical_not(owns_tile))))
    def _():
        ys_ref[...] = jnp.zeros_like(ys_ref)

    for sb in range(tmx // sub):
        r0 = t0 + sb * sub
        overlaps = jnp.logical_and(hi > r0, lo < r0 + sub)

        @pl.when(jnp.logical_and(overlaps, jnp.logical_not(owns_tile)))
        def _(sb=sb, r0=r0):
            y = mlp(sb * sub, sub)
            whole = jnp.logical_and(lo <= r0, hi >= r0 + sub)

            @pl.when(whole)
            def _():
                _to_row_tiles(ys_ref, sb * sub * ROW_CHUNKS, y)

            @pl.when(jnp.logical_not(whole))
            def _():
                _to_row_tiles_range(ys_ref, sb * sub * ROW_CHUNKS, y, tmp_ref, r0, lo, hi)


def _experts(plan, xs, wgu, bgu, wd, bd, tmx, sub):
    n_rows = xs.shape[0] // ROW_CHUNKS
    n_work = plan[0].shape[0]
    _, d, de2 = wgu.shape
    de = wd.shape[1]
    kern = functools.partial(_experts_kernel, tmx=tmx, sub=sub)
    grid_spec = pltpu.PrefetchScalarGridSpec(
        num_scalar_prefetch=len(plan),
        grid=(n_work,),
        in_specs=[
            pl.BlockSpec((tmx * ROW_CHUNKS, V7X_LANES), lambda i, tile, *_: (tile[i], 0)),
            pl.BlockSpec(memory_space=pl.ANY),
            pl.BlockSpec((1, 1, de2), lambda i, tile, ex, *_: (ex[i], 0, 0)),
            pl.BlockSpec(memory_space=pl.ANY),
            pl.BlockSpec((1, 1, d), lambda i, tile, ex, *_: (ex[i], 0, 0)),
        ],
        out_specs=pl.BlockSpec((tmx * ROW_CHUNKS, V7X_LANES), lambda i, tile, *_: (tile[i], 0)),
        scratch_shapes=[pltpu.VMEM((d, de2), BF16), pltpu.VMEM((de, d), BF16),
                        pltpu.VMEM((d, de2), F32), pltpu.VMEM((de, d), F32),
                        pltpu.VMEM((sub * ROW_CHUNKS, V7X_LANES), F32),
                        pltpu.SemaphoreType.DMA((2,))],
    )
    return pl.pallas_call(
        kern,
        grid_spec=grid_spec,
        out_shape=jax.ShapeDtypeStruct((n_rows * ROW_CHUNKS, V7X_LANES), F32),
        compiler_params=pltpu.CompilerParams(
            dimension_semantics=("arbitrary",), vmem_limit_bytes=VMEM_LIMIT),
        name="experts",
    )(*plan, xs, wgu, bgu, wd, bd)


def _combine_kernel(pos_ref, posn_ref, ys_hbm, xp_ref, xs_ref, plep_ref, ples_ref, wtok_ref,
                    gple_ref, wgate_ref, gfin_ref,
                    yp_ref, ysm_ref, gath_ref, ytmp_ref, sems, *, n_p_tiles, n_tiles):
    i = pl.program_id(0)
    tc = xp_ref.shape[0]
    slot = i % 2
    slot_rows = TOP_K * tc

    def gather(p_ref, s):
        def issue_token(t):
            for k in range(TOP_K):
                _row_copy(ys_hbm, p_ref[k, t], gath_ref, s * slot_rows + k * tc + t,
                          sems.at[s]).start(priority=k % 2)

        _issue_rows(tc, issue_token)

    @pl.when(i == 0)
    def _():
        gather(pos_ref, slot)

    is_p = i < n_p_tiles
    base = slot * (slot_rows * ROW_CHUNKS)
    pltpu.make_async_copy(ys_hbm.at[pl.ds(0, slot_rows * ROW_CHUNKS), :],
                          gath_ref.at[pl.ds(base, slot_rows * ROW_CHUNKS), :], sems.at[slot]).wait()

    sub = min(COMBINE_SUB_ROWS, tc)

    def sub_tile(r0):
        for t in range(r0, r0 + sub):
            for k in range(TOP_K):
                _row_copy(ys_hbm, posn_ref[k, t], gath_ref, (1 - slot) * slot_rows + k * tc + t,
                          sems.at[1 - slot]).start(priority=k % 2)
        rows = pl.ds(r0, sub)
        x1 = jnp.where(is_p, xp_ref[rows, :], xs_ref[rows, :])
        ple = jnp.where(is_p, plep_ref[rows, :], ples_ref[rows, :])
        wt = wtok_ref[rows, :]
        moe = wt[:, 0:1] * _from_row_tiles(gath_ref, base + r0 * ROW_CHUNKS, sub)
        for k in range(1, TOP_K):
            moe = moe + wt[:, k:k + 1] * _from_row_tiles(gath_ref, base + (k * tc + r0) * ROW_CHUNKS, sub)
        x2 = x1 + moe
        gate = _sigmoid(_dot(_rmsnorm(x2, gple_ref[...]).astype(BF16), wgate_ref[...]))
        ytmp_ref[rows, :] = _rmsnorm(x2 + ple * gate, gfin_ref[...])

    for r0 in range(0, tc, sub):
        sub_tile(r0)

    @pl.when(is_p)
    def _():
        yp_ref[...] = ytmp_ref[...]

    @pl.when(jnp.logical_not(is_p))
    def _():
        ysm_ref[...] = ytmp_ref[...]

    @pl.when(i == n_tiles - 1)
    def _():
        spare = (1 - slot) * (slot_rows * ROW_CHUNKS)
        pltpu.make_async_copy(ys_hbm.at[pl.ds(0, slot_rows * ROW_CHUNKS), :],
                              gath_ref.at[pl.ds(spare, slot_rows * ROW_CHUNKS), :], sems.at[1 - slot]).wait()


def _combine(pos, ys, x1p, x1s, plep, ples, wtok, consts, tc):
    t_p, d = x1p.shape
    t_s = x1s.shape[0]
    n_p_tiles, n_s_tiles = t_p // tc, t_s // tc
    n_tiles = n_p_tiles + n_s_tiles
    kern = functools.partial(_combine_kernel, n_p_tiles=n_p_tiles, n_tiles=n_tiles)
    return pl.pallas_call(
        kern,
        grid=(n_tiles,),
        in_specs=[pl.BlockSpec((TOP_K, tc), lambda i: (0, i), memory_space=pltpu.SMEM),
                  pl.BlockSpec((TOP_K, tc), lambda i: (0, jnp.minimum(i + 1, n_tiles - 1)),
                               memory_space=pltpu.SMEM),
                  pl.BlockSpec(memory_space=pl.ANY)]
        + _two_source_specs(tc, d, n_p_tiles) + _two_source_specs(tc, d, n_p_tiles)
        + [pl.BlockSpec((tc, V7X_LANES), lambda i: (i, 0))] + [_const_spec(c.shape) for c in consts],
        out_specs=tuple(_two_source_specs(tc, d, n_p_tiles)),
        out_shape=(jax.ShapeDtypeStruct((t_p, d), F32), jax.ShapeDtypeStruct((t_s, d), F32)),
        scratch_shapes=[pltpu.VMEM((2 * TOP_K * tc * ROW_CHUNKS, V7X_LANES), F32),
                        pltpu.VMEM((tc, d), F32),
                        pltpu.SemaphoreType.DMA((2,))],
        compiler_params=pltpu.CompilerParams(
            dimension_semantics=("arbitrary",), vmem_limit_bytes=VMEM_LIMIT),
        name="combine",
    )(pos, pos, ys, x1p, x1s, plep, ples, wtok, *consts)


def _plan(idx, rank, counts, tmx, n_work):
    eids = jnp.arange(N_EXPERTS, dtype=I32)
    incl = eids[None, :] <= eids[:, None]
    ends = jnp.sum(jnp.where(incl, counts[None, :], 0), axis=1)
    offs = ends - counts
    pos = rank + jnp.sum(jnp.where(idx[..., None] == eids, offs, 0), axis=-1)

    first_tile = offs // tmx
    last_tile = (ends - 1) // tmx
    n_e = jnp.where(counts > 0, last_tile - first_tile + 1, 0)
    iend = jnp.sum(jnp.where(incl, n_e[None, :], 0), axis=1)
    istart = iend - n_e
    total = iend[N_EXPERTS - 1]
    i = jnp.arange(n_work, dtype=I32)
    ic = jnp.minimum(i, total - 1)
    e_i = jnp.sum((ic[:, None] >= iend[None, :]).astype(I32), axis=1)
    onehot = e_i[:, None] == eids[None, :]
    pick = lambda v: jnp.sum(jnp.where(onehot, v[None, :], 0), axis=1)
    tile_i = pick(first_tile) + (ic - pick(istart))
    valid = i < total
    lo = jnp.where(valid, jnp.maximum(pick(offs), tile_i * tmx), 0)
    hi = jnp.where(valid, jnp.minimum(pick(ends), (tile_i + 1) * tmx), 0)
    prev_e = jnp.concatenate([jnp.full((1,), -1, I32), e_i[:-1]])
    prev_tile = jnp.concatenate([jnp.full((1,), -1, I32), tile_i[:-1]])
    wchg = (e_i != prev_e).astype(I32)
    first = (tile_i != prev_tile).astype(I32)
    later = jnp.logical_and(eids[None, :] > e_i[:, None], counts[None, :] > 0)
    nxt = jnp.min(jnp.where(later, eids[None, :], N_EXPERTS), axis=1)
    nxt = jnp.where(nxt < N_EXPERTS, nxt, -1)
    return pos.astype(I32), tuple(v.astype(I32) for v in (tile_i, e_i, lo, hi, wchg, first, nxt))


def _block_diag_gates(w_a, w_x):
    heads, hd, _ = w_a.shape
    per = GATE_GROUP // hd
    groups = heads // per
    eye = jnp.eye(per, dtype=w_a.dtype)

    def bd(w):
        w4 = w.reshape(groups, per, hd, hd)
        return jnp.einsum('ghij,hk->ghikj', w4, eye).reshape(groups, GATE_GROUP, GATE_GROUP)

    return jnp.concatenate([bd(w_a), bd(w_x)], axis=-1).astype(BF16)


def _pick_tile(pref, *sizes):
    t = pref
    while any(s % t for s in sizes):
        t //= 2
    return t


def kernel(x_prompt, x_sample, p_prompt, p_sample, state_lru_h, state_conv, state_pool, g_mix, w_in, conv_w, conv_b, w_rg_a, b_rg_a, w_rg_x, b_rg_x, lru_lambda, pool_w, pool_scale, w_proj_a, w_proj_b, w_out, g_moe, w_router, b_router, w_gate_up, b_gate_up, w_down, b_down, g_ple, w_ple_gate, w_ple, g_ple_post, g_final):
    depth = g_mix.shape[0]
    assert depth == 1, "single-layer trunk"
    nb, seq, d = x_prompt.shape
    sb, sl, _ = x_sample.shape
    t_p, t_s = nb * seq, sb * sl
    t = t_p + t_s
    row = lambda v: v.reshape(1, -1)

    mixer_wts = (
        row(g_mix[0]), w_in[0].astype(BF16), conv_w[0], row(conv_b[0]),
        _block_diag_gates(w_rg_a[0], w_rg_x[0]), row(b_rg_a[0]), row(b_rg_x[0]), row(lru_lambda[0]),
        pool_w[0].astype(BF16), row(pool_scale[0]),
        w_proj_a[0].astype(BF16), w_proj_b[0].astype(BF16), w_out[0].astype(BF16),
    )

    tl = _pick_tile(256, seq)
    x1p, h_p, conv_p, pool_p = _mixer_prompt(x_prompt, mixer_wts, tl)
    x1p = x1p.reshape(t_p, d)
    x1s, h_s, conv_s, pool_s = _mixer_sample(
        x_sample, state_lru_h, state_conv, state_pool, mixer_wts, sl)

    wr = jnp.pad(w_router[0], ((0, 0), (0, V7X_LANES - N_EXPERTS)))
    wr_hi = wr.astype(BF16)
    wr_lo = (wr - wr_hi.astype(F32)).astype(BF16)
    br = jnp.pad(b_router[0], (0, V7X_LANES - N_EXPERTS)).reshape(1, -1)
    tm = _pick_tile(512, t_p, t_s)
    idx, rank, wtok, cnt = _router(x1p, x1s, row(g_moe[0]), wr_hi, wr_lo, br, tm)

    tmx = _pick_tile(512, TOP_K * t)
    n_work = (TOP_K * t) // tmx + N_EXPERTS - 1
    pos, plan = _plan(idx, rank, cnt[:, 0], tmx, n_work)

    pp = p_prompt[0].reshape(t_p, -1)
    ps = jnp.swapaxes(p_sample[0], 0, 1).reshape(t_s, -1)
    xs, plep, ples = _dispatch(pos, x1p, x1s, pp, ps, row(g_moe[0]), w_ple[0].astype(BF16),
                               row(g_ple_post[0]), tm)
    ys = _experts(plan, xs, w_gate_up[0], b_gate_up[0][:, None, :], w_down[0], b_down[0][:, None, :], tmx,
                  _pick_tile(EXPERT_SUB_ROWS, tmx))

    tc = _pick_tile(512, t_p, t_s)
    consts = (row(g_ple[0]), w_ple_gate[0].astype(BF16), row(g_final))
    y_p, y_s = _combine(pos, ys, x1p, x1s, plep, ples, wtok, consts, tc)

    y_prompt = y_p.reshape(nb, seq, d)
    y_sample = jnp.swapaxes(y_s.reshape(sl, sb, d), 0, 1)
    return (y_prompt, y_sample,
            h_p.reshape(depth, nb, d), conv_p[None], pool_p[None],
            h_s, conv_s, pool_s)
```
